```python
import math
import jax, jax.numpy as jnp
from jax import lax
import numpy as np

D_MODEL = 1024
BATCH = 4
SEQ = 8192
DEPTH = 1

GRID_W = 64
N_META = 16
NA_HEADS = 8
NA_HEAD_DIM = 64
NA_WIN_ROWS = 8
NA_WIN_COLS = 16
DIFF_HEADS = 4
DIFF_HEAD_DIM = 64
NA_WIDTH = NA_HEADS * NA_HEAD_DIM
DIFF_WIDTH = DIFF_HEADS * 2 * DIFF_HEAD_DIM
N_BRANCH = 2
IN_SPLITS = (NA_WIDTH, NA_WIDTH, NA_WIDTH, DIFF_WIDTH, DIFF_WIDTH, DIFF_WIDTH, D_MODEL, D_MODEL)
IN_COLS = sum(IN_SPLITS)
D_FF = -(-(8 * D_MODEL) // (3 * 256)) * 256
Q_BLOCK = 128
ROPE_THETA = 10000.0
NORM_EPS = 1e-6
SUBLN_EPS = 1e-5

kernel_name = "hybrid_natten_diffattn_gated_block"


def rmsnorm(x, g, eps=NORM_EPS):
    xf = x.astype(jnp.float32)
    y = xf * lax.rsqrt(jnp.mean(xf * xf, axis=-1, keepdims=True) + eps)
    return (y * g.astype(jnp.float32)).astype(x.dtype)


def rope(x, pos):
    d = x.shape[-1]
    half = d // 2
    inv = ROPE_THETA ** (-jnp.arange(half, dtype=jnp.float32) / half)
    ang = pos.astype(jnp.float32)[:, None] * inv[None, :]
    shp = (1, pos.shape[0]) + (1,) * (x.ndim - 3) + (half,)
    cos = jnp.cos(ang).reshape(shp)
    sin = jnp.sin(ang).reshape(shp)
    xf = x.astype(jnp.float32)
    x1, x2 = xf[..., :half], xf[..., half:]
    return jnp.concatenate([x1 * cos - x2 * sin, x2 * cos + x1 * sin], axis=-1).astype(x.dtype)


def neighbourhood_attention(q, k, v, rpb):
    B, L, H, d = q.shape
    T = L - N_META
    rows = T // GRID_W
    kr = min(NA_WIN_ROWS, rows)
    kc = NA_WIN_COLS
    scale = d ** -0.5
    qm, km, vm = q[:, :N_META], k[:, :N_META], v[:, :N_META]
    kg = k[:, N_META:].reshape(B, rows, GRID_W, H, d)
    vg = v[:, N_META:].reshape(B, rows, GRID_W, H, d)
    qg = q[:, N_META:].reshape(B, rows, GRID_W, H, d)

    s_mm = jnp.einsum('bqhd,bkhd->bhqk', qm, km).astype(jnp.float32) * scale
    p_mm = jax.nn.softmax(s_mm, axis=-1).astype(v.dtype)
    out_meta = jnp.einsum('bhqk,bkhd->bqhd', p_mm, vm)

    cols = jnp.arange(GRID_W)
    col_start = jnp.clip(cols - kc // 2, 0, GRID_W - kc)
    col_idx = col_start[:, None] + jnp.arange(kc)[None, :]
    col_bias_idx = col_idx - cols[:, None] + (NA_WIN_COLS - 1)
    bias_c = rpb.astype(jnp.float32)

    def row_fn(args):
        r, q_row = args
        rs = jnp.clip(r - kr // 2, 0, rows - kr)
        k_rows = lax.dynamic_slice_in_dim(kg, rs, kr, axis=1)
        v_rows = lax.dynamic_slice_in_dim(vg, rs, kr, axis=1)
        k_win = k_rows[:, :, col_idx]
        v_win = v_rows[:, :, col_idx]
        s_win = jnp.einsum('bwhd,biwjhd->bhwij', q_row, k_win).astype(jnp.float32) * scale
        row_bias_idx = rs + jnp.arange(kr) - r + (NA_WIN_ROWS - 1)
        bias = bias_c[:, row_bias_idx[None, :, None], col_bias_idx[:, None, :]]
        s_win = (s_win + bias[None]).reshape(B, H, GRID_W, kr * kc)
        s_meta = jnp.einsum('bwhd,bkhd->bhwk', q_row, km).astype(jnp.float32) * scale
        p = jax.nn.softmax(jnp.concatenate([s_win, s_meta], axis=-1), axis=-1).astype(v.dtype)
        p_win = p[..., :kr * kc].reshape(B, H, GRID_W, kr, kc)
        p_meta = p[..., kr * kc:]
        return (jnp.einsum('bhwij,biwjhd->bwhd', p_win, v_win)
                + jnp.einsum('bhwk,bkhd->bwhd', p_meta, vm))

    out_rows = lax.map(row_fn, (jnp.arange(rows), jnp.moveaxis(qg, 1, 0)))
    out_grid = jnp.moveaxis(out_rows, 0, 1).reshape(B, T, H, d)
    return jnp.concatenate([out_meta, out_grid], axis=1)


def diff_attention(q, k, v, lam, lambda_init, subln_g):
    B, L, H, _, d = q.shape
    T = L - N_META
    scale = d ** -0.5

    def attend(q_blk):
        s = jnp.einsum('bqhcd,bkhcd->bhcqk', q_blk, k).astype(jnp.float32) * scale
        p = jax.nn.softmax(s, axis=-1)
        a = p[:, :, 0] - lam * p[:, :, 1]
        return jnp.einsum('bhqk,bkhe->bqhe', a.astype(v.dtype), v)

    out_meta = attend(q[:, :N_META])
    q_blocks = jnp.moveaxis(q[:, N_META:].reshape(B, T // Q_BLOCK, Q_BLOCK, H, 2, d), 1, 0)
    out_real = lax.map(attend, q_blocks)
    out_real = jnp.moveaxis(out_real, 0, 1).reshape(B, T, H, 2 * d)
    o = jnp.concatenate([out_meta, out_real], axis=1)
    return rmsnorm(o, subln_g, SUBLN_EPS) * (1.0 - lambda_init)


def hybrid_layer(x, pos, layer_idx, mix_norm, w_in, na_rpb, lambda_q1, lambda_k1, lambda_q2,
                 lambda_k2, diff_subln, w_na_out, w_diff_out, w_o, ffn_norm, w_gate, w_up, w_down):
    B, L, _ = x.shape
    h = rmsnorm(x, mix_norm)
    proj = h @ w_in
    na_q, na_k, na_v, df_q, df_k, df_v, g_na, g_df = jnp.split(
        proj, list(np.cumsum(IN_SPLITS)[:-1]), axis=-1)

    na_out = neighbourhood_attention(
        na_q.reshape(B, L, NA_HEADS, NA_HEAD_DIM),
        na_k.reshape(B, L, NA_HEADS, NA_HEAD_DIM),
        na_v.reshape(B, L, NA_HEADS, NA_HEAD_DIM), na_rpb)
    o_na = na_out.reshape(B, L, NA_WIDTH) @ w_na_out

    lambda_init = 0.8 - 0.6 * math.exp(-0.3 * layer_idx)
    lam = (jnp.exp(jnp.sum(lambda_q1.astype(jnp.float32) * lambda_k1.astype(jnp.float32)))
           - jnp.exp(jnp.sum(lambda_q2.astype(jnp.float32) * lambda_k2.astype(jnp.float32)))
           + lambda_init)
    dq = rope(df_q.reshape(B, L, DIFF_HEADS, 2, DIFF_HEAD_DIM), pos)
    dk = rope(df_k.reshape(B, L, DIFF_HEADS, 2, DIFF_HEAD_DIM), pos)
    dv = df_v.reshape(B, L, DIFF_HEADS, 2 * DIFF_HEAD_DIM)
    df_out = diff_attention(dq, dk, dv, lam, lambda_init, diff_subln)
    o_df = df_out.reshape(B, L, DIFF_WIDTH) @ w_diff_out

    merged = jax.nn.sigmoid(g_na) * o_na + jax.nn.sigmoid(g_df) * o_df
    x = x + merged @ w_o

    h = rmsnorm(x, ffn_norm)
    x = x + (jax.nn.silu(h @ w_gate) * (h @ w_up)) @ w_down
    return x


def setup_inputs(seed: int = 0) -> dict:
    key = jax.random.key(seed)
    ks = jax.random.split(key, 20)
    f32 = jnp.float32

    def nrm(k, shape, scale):
        return jax.random.normal(k, shape, f32) * scale

    return {
        "x": nrm(ks[0], (BATCH, SEQ, D_MODEL), 1.0),
        "meta_tokens": nrm(ks[1], (N_META, D_MODEL), 1.0),
        "mix_norm": 1.0 + nrm(ks[2], (DEPTH, D_MODEL), 0.02),
        "w_in": nrm(ks[3], (DEPTH, D_MODEL, IN_COLS), D_MODEL ** -0.5),
        "na_rpb": nrm(ks[4], (DEPTH, NA_HEADS, 2 * NA_WIN_ROWS - 1, 2 * NA_WIN_COLS - 1), 0.1),
        "lambda_q1": nrm(ks[5], (DEPTH, DIFF_HEAD_DIM), 0.1),
        "lambda_k1": nrm(ks[6], (DEPTH, DIFF_HEAD_DIM), 0.1),
        "lambda_q2": nrm(ks[7], (DEPTH, DIFF_HEAD_DIM), 0.1),
        "lambda_k2": nrm(ks[8], (DEPTH, DIFF_HEAD_DIM), 0.1),
        "diff_subln": 1.0 + nrm(ks[9], (DEPTH, 2 * DIFF_HEAD_DIM), 0.02),
        "w_na_out": nrm(ks[10], (DEPTH, NA_WIDTH, D_MODEL), NA_WIDTH ** -0.5),
        "w_diff_out": nrm(ks[11], (DEPTH, DIFF_WIDTH, D_MODEL), DIFF_WIDTH ** -0.5),
        "w_o": nrm(ks[12], (DEPTH, D_MODEL, D_MODEL), D_MODEL ** -0.5),
        "ffn_norm": 1.0 + nrm(ks[13], (DEPTH, D_MODEL), 0.02),
        "w_gate": nrm(ks[14], (DEPTH, D_MODEL, D_FF), D_MODEL ** -0.5),
        "w_up": nrm(ks[15], (DEPTH, D_MODEL, D_FF), D_MODEL ** -0.5),
        "w_down": nrm(ks[16], (DEPTH, D_FF, D_MODEL), D_FF ** -0.5),
        "final_norm": 1.0 + nrm(ks[17], (D_MODEL,), 0.02),
    }


def reference(x, meta_tokens, mix_norm, w_in, na_rpb, lambda_q1, lambda_k1, lambda_q2, lambda_k2,
              diff_subln, w_na_out, w_diff_out, w_o, ffn_norm, w_gate, w_up, w_down, final_norm):
    B = x.shape[0]
    meta = jnp.broadcast_to(meta_tokens[None].astype(x.dtype), (B, N_META, D_MODEL))
    h = jnp.concatenate([meta, x], axis=1)
    pos = jnp.arange(h.shape[1], dtype=jnp.int32)
    for l in range(DEPTH):
        h = hybrid_layer(h, pos, l, mix_norm[l], w_in[l], na_rpb[l], lambda_q1[l], lambda_k1[l],
                         lambda_q2[l], lambda_k2[l], diff_subln[l], w_na_out[l], w_diff_out[l],
                         w_o[l], ffn_norm[l], w_gate[l], w_up[l], w_down[l])
    h = rmsnorm(h, final_norm)
    return h[:, N_META:]
```

```python
import functools
import math

import jax
import jax.numpy as jnp
from jax import lax
from jax.experimental import pallas as pl
from jax.experimental.pallas import tpu as pltpu

D_MODEL = 1024
GRID_W = 64
N_META = 16
NA_HEADS = 8
NA_HEAD_DIM = 64
NA_WIN_ROWS = 8
NA_WIN_COLS = 16
DIFF_HEADS = 4
DIFF_HEAD_DIM = 64
NA_WIDTH = NA_HEADS * NA_HEAD_DIM
DIFF_WIDTH = DIFF_HEADS * 2 * DIFF_HEAD_DIM
D_FF = 2816
ROPE_THETA = 10000.0
NORM_EPS = 1e-6
SUBLN_EPS = 1e-5
LAMBDA_INIT = 0.8 - 0.6 * math.exp(-0.3 * 0)

CHUNK = 512
NA_GROUP_ROWS = CHUNK // GRID_W
NA_KEY_ROWS = 16
NA_KEY_BLOCK = 256
MASK_VALUE = -1e30
VMEM_LIMIT_BYTES = 56 * 1024 * 1024

BF16 = jnp.bfloat16
F32 = jnp.float32


def _dot(a, b):
    return jnp.dot(a, b, preferred_element_type=F32)


def _dot_nt(a, b):
    return lax.dot_general(a, b, (((1,), (1,)), ((), ())), preferred_element_type=F32)


def _dot_tn(a, b):
    return lax.dot_general(a, b, (((0,), (0,)), ((), ())), preferred_element_type=F32)


def _rms_scale(x, eps):
    return x * lax.rsqrt(jnp.mean(x * x, axis=-1, keepdims=True) + eps)


def _in_proj_kernel(x_ref, g_ref, wt_ref, ws_ref, cos_t_ref, sin_t_ref, cos_s_ref, sin_s_ref,
                    na_qt_ref, na_vt_ref, df_qt_ref, df_vt_ref, na_k_ref, df_k_ref,
                    g_na_ref, g_df_ref):
    h = (_rms_scale(x_ref[0], NORM_EPS) * g_ref[...]).astype(BF16)

    na_qt_ref[0, 0] = (_dot_nt(wt_ref[0:512], h) * (NA_HEAD_DIM ** -0.5)).astype(BF16)
    na_vt_ref[0, 0] = _dot_nt(wt_ref[512:1024], h).astype(BF16)
    df_vt_ref[0, 0] = _dot_nt(wt_ref[1536:2048], h).astype(BF16)
    dq = _dot_nt(wt_ref[1024:1536], h) * (DIFF_HEAD_DIM ** -0.5)
    cos_t = cos_t_ref[...]
    sin_t = sin_t_ref[...]
    half = DIFF_HEAD_DIM // 2
    for c in range(DIFF_WIDTH // DIFF_HEAD_DIM):
        x1 = dq[c * 64:c * 64 + half]
        x2 = dq[c * 64 + half:(c + 1) * 64]
        df_qt_ref[0, 0, c * 64:c * 64 + half, :] = (x1 * cos_t - x2 * sin_t).astype(BF16)
        df_qt_ref[0, 0, c * 64 + half:(c + 1) * 64, :] = (x2 * cos_t + x1 * sin_t).astype(BF16)

    na_k_ref[0] = _dot(h, ws_ref[:, 0:512]).astype(BF16)
    dk = _dot(h, ws_ref[:, 512:1024])
    cos_s = cos_s_ref[...]
    sin_s = sin_s_ref[...]
    lane = lax.broadcasted_iota(jnp.int32, cos_s.shape, 1)
    low_half = (lane % DIFF_HEAD_DIM) < half
    for c in range(DIFF_WIDTH // 128):
        xk = dk[:, c * 128:(c + 1) * 128]
        partner = jnp.where(low_half, pltpu.roll(xk, 128 - half, 1), pltpu.roll(xk, half, 1))
        df_k_ref[0, :, c * 128:(c + 1) * 128] = (xk * cos_s + partner * sin_s).astype(BF16)
    g_na_ref[0] = jax.nn.sigmoid(_dot(h, ws_ref[:, 1024:2048])).astype(BF16)
    g_df_ref[0] = jax.nn.sigmoid(_dot(h, ws_ref[:, 2048:3072])).astype(BF16)


def _in_proj(x, gain, w_t, w_s, cos_t, sin_t, cos_s, sin_s, rows):
    b, t, _ = x.shape
    n = t // rows
    const = lambda shape: pl.BlockSpec(shape, lambda i, j: (0,) * len(shape),
                                       pipeline_mode=pl.Buffered(1))
    feat_major = jax.ShapeDtypeStruct((b, n, 512, rows), BF16)
    feat_spec = pl.BlockSpec((1, 1, 512, rows), lambda i, j: (i, j, 0, 0))
    tok = lambda width: jax.ShapeDtypeStruct((b, t, width), BF16)
    tok_spec = lambda width: pl.BlockSpec((1, rows, width), lambda i, j: (i, j, 0))
    return pl.pallas_call(
        _in_proj_kernel,
        grid=(b, n),
        in_specs=[
            pl.BlockSpec((1, rows, D_MODEL), lambda i, j: (i, j, 0)),
            const((1, D_MODEL)),
            const((2048, D_MODEL)),
            const((D_MODEL, 3072)),
            pl.BlockSpec((DIFF_HEAD_DIM // 2, rows), lambda i, j: (0, j)),
            pl.BlockSpec((DIFF_HEAD_DIM // 2, rows), lambda i, j: (0, j)),
            pl.BlockSpec((rows, 128), lambda i, j: (j, 0)),
            pl.BlockSpec((rows, 128), lambda i, j: (j, 0)),
        ],
        out_specs=[feat_spec, feat_spec, feat_spec, feat_spec,
                   tok_spec(512), tok_spec(512), tok_spec(D_MODEL), tok_spec(D_MODEL)],
        out_shape=[feat_major, feat_major, feat_major, feat_major,
                   tok(512), tok(512), tok(D_MODEL), tok(D_MODEL)],
        compiler_params=pltpu.CompilerParams(
            dimension_semantics=("parallel", "parallel"), vmem_limit_bytes=VMEM_LIMIT_BYTES),
        name="in_proj",
    )(x, gain, w_t, w_s, cos_t, sin_t, cos_s, sin_s)


def _rope_tables(pos):
    half = DIFF_HEAD_DIM // 2
    inv = ROPE_THETA ** (-jnp.arange(half, dtype=F32) / half)
    ang = pos.astype(F32)[:, None] * inv[None, :]
    cos, sin = jnp.cos(ang), jnp.sin(ang)
    cos_s = jnp.tile(cos, (1, 128 // half))
    sin_s = jnp.tile(jnp.concatenate([-sin, sin], axis=1), (1, 128 // DIFF_HEAD_DIM))
    return cos.T, sin.T, cos_s, sin_s


def _na_bias_tiles(rpb):
    w = jnp.arange(GRID_W)
    c = jnp.arange(GRID_W)
    col_start = jnp.clip(w - NA_WIN_COLS // 2, 0, GRID_W - NA_WIN_COLS)
    col_ok = (c[:, None] >= col_start[None, :]) & (c[:, None] < col_start[None, :] + NA_WIN_COLS)
    col_idx = jnp.clip(c[:, None] - w[None, :] + NA_WIN_COLS - 1, 0, 2 * NA_WIN_COLS - 2)
    per_dr = rpb.astype(F32)[:, :, col_idx]
    per_dr = jnp.where(col_ok[None, None], per_dr, MASK_VALUE)
    masked = jnp.full((rpb.shape[0], GRID_W, GRID_W), MASK_VALUE, F32)

    def tile(dr_of, ok_of):
        cols = []
        for j in range(NA_GROUP_ROWS):
            blocks = [per_dr[:, dr_of(i, j) + NA_WIN_ROWS - 1] if ok_of(i, j) else masked
                      for i in range(NA_KEY_ROWS)]
            cols.append(jnp.concatenate(blocks, axis=1))
        return jnp.concatenate(cols, axis=2)

    first = tile(lambda i, j: i - j, lambda i, j: max(j - 4, 0) <= i < max(j - 4, 0) + NA_WIN_ROWS)
    inner = tile(lambda i, j: i - j - 4, lambda i, j: 0 <= i - j < NA_WIN_ROWS)
    last = tile(lambda i, j: i - j - 8,
                lambda i, j: min(j + 4, 8) <= i < min(j + 4, 8) + NA_WIN_ROWS)
    return jnp.stack([first, inner, last])


def _na_kernel(qt_ref, k0_ref, k1_ref, k2_ref, k3_ref, v0_ref, v1_ref, v2_ref, v3_ref,
               km_ref, vm_ref, bias_ref, o_ref):
    qt = qt_ref[0, 0]
    k_all = jnp.concatenate([k0_ref[0], k1_ref[0], k2_ref[0], k3_ref[0]], axis=0)
    vt_all = jnp.concatenate([v0_ref[0, 0], v1_ref[0, 0], v2_ref[0, 0], v3_ref[0, 0]], axis=1)
    k_meta = km_ref[...]
    vt_meta = vm_ref[...]
    row = lax.broadcasted_iota(jnp.int32, qt.shape, 0)
    outs = []
    for hh in range(2):
        in_head = (row >= hh * NA_HEAD_DIM) & (row < (hh + 1) * NA_HEAD_DIM)
        qt_h = jnp.where(in_head, qt, jnp.zeros_like(qt))
        s = _dot(k_all, qt_h) + bias_ref[0, hh]
        s_meta = _dot(k_meta, qt_h)
        m = jnp.maximum(jnp.max(s, axis=0, keepdims=True), jnp.max(s_meta, axis=0, keepdims=True))
        p = jnp.exp(s - m)
        p_meta = jnp.exp(s_meta - m)
        l = jnp.sum(p, axis=0, keepdims=True) + jnp.sum(p_meta, axis=0, keepdims=True)
        o = _dot(vt_all, p.astype(BF16)) + _dot(vt_meta, p_meta.astype(BF16))
        outs.append(o / l)
    o_ref[0, 0] = jnp.where(row < NA_HEAD_DIM, outs[0], outs[1]).astype(BF16)


def _na_attention(qt, k, vt, k_meta, vt_meta, bias):
    b, n_groups, _, _ = qt.shape
    n_kblocks = n_groups * (CHUNK // NA_KEY_BLOCK)
    per_chunk = CHUNK // NA_KEY_BLOCK

    def kblock(g, i):
        return jnp.clip(2 * g - 1, 0, n_kblocks - NA_KEY_ROWS * GRID_W // NA_KEY_BLOCK) + i

    def variant(g):
        return jnp.where(g == 0, 0, jnp.where(g == n_groups - 1, 2, 1))

    k_specs = [pl.BlockSpec((1, NA_KEY_BLOCK, 128),
                            functools.partial(lambda i, hp, g, bi: (bi, kblock(g, i), hp), i))
               for i in range(4)]
    v_specs = [pl.BlockSpec((1, 1, 128, NA_KEY_BLOCK),
                            functools.partial(
                                lambda i, hp, g, bi: (bi, kblock(g, i) // per_chunk, hp,
                                                      kblock(g, i) % per_chunk), i))
               for i in range(4)]
    return pl.pallas_call(
        _na_kernel,
        grid=(NA_HEADS // 2, n_groups, b),
        in_specs=[pl.BlockSpec((1, 1, 128, CHUNK), lambda hp, g, bi: (bi, g, hp, 0))]
        + k_specs + v_specs + [
            pl.BlockSpec((N_META, 128), lambda hp, g, bi: (0, hp)),
            pl.BlockSpec((128, N_META), lambda hp, g, bi: (hp, 0)),
            pl.BlockSpec((1, 2, NA_KEY_ROWS * GRID_W, CHUNK),
                         lambda hp, g, bi: (variant(g), hp, 0, 0)),
        ],
        out_specs=pl.BlockSpec((1, 1, 128, CHUNK), lambda hp, g, bi: (bi, g, hp, 0)),
        out_shape=jax.ShapeDtypeStruct(qt.shape, BF16),
        compiler_params=pltpu.CompilerParams(
            dimension_semantics=("parallel", "parallel", "parallel"),
            vmem_limit_bytes=VMEM_LIMIT_BYTES),
        name="na_attn",
    )(qt, k, k, k, k, vt, vt, vt, vt, k_meta, vt_meta, bias)


def _df_kernel(qt_ref, k_ref, vt_ref, km_ref, vm_ref, lam_ref, g_ref, o_ref, *, n_chunks):
    qt = qt_ref[0, 0]
    row = lax.broadcasted_iota(jnp.int32, qt.shape, 0)
    zero = jnp.zeros_like(qt)
    qts = (jnp.where(row < DIFF_HEAD_DIM, qt, zero), jnp.where(row >= DIFF_HEAD_DIM, qt, zero))

    k_meta = km_ref[...]
    vt_meta = vm_ref[...]
    state = []
    for c in range(2):
        s = _dot(k_meta, qts[c])
        m = jnp.max(s, axis=0, keepdims=True)
        p = jnp.exp(s - m)
        state += [m, jnp.sum(p, axis=0, keepdims=True), _dot(vt_meta, p.astype(BF16))]

    def body(i, carry):
        kc = k_ref[0, i]
        vc = vt_ref[0, i]
        new = []
        for c in range(2):
            m_old, l_old, acc_old = carry[3 * c:3 * c + 3]
            s = _dot(kc, qts[c])
            m_new = jnp.maximum(m_old, jnp.max(s, axis=0, keepdims=True))
            alpha = jnp.exp(m_old - m_new)
            p = jnp.exp(s - m_new)
            l_new = alpha * l_old + jnp.sum(p, axis=0, keepdims=True)
            acc_new = alpha * acc_old + _dot(vc, p.astype(BF16))
            new += [m_new, l_new, acc_new]
        return tuple(new)

    m0, l0, acc0, m1, l1, acc1 = lax.fori_loop(0, n_chunks, body, tuple(state))

    lam_v = lam_ref[...]
    lam = (jnp.exp(jnp.sum(lam_v[0:1] * lam_v[1:2], axis=1, keepdims=True))
           - jnp.exp(jnp.sum(lam_v[2:3] * lam_v[3:4], axis=1, keepdims=True)) + LAMBDA_INIT)
    o = acc0 / l0 - lam * (acc1 / l1)
    o = o * lax.rsqrt(jnp.mean(o * o, axis=0, keepdims=True) + SUBLN_EPS)
    o_ref[0, 0] = (o * g_ref[...] * (1.0 - LAMBDA_INIT)).astype(BF16)


def _df_attention(qt, k, vt, k_meta, vt_meta, lam_vecs, subln_col):
    b, n_chunks, _, _ = qt.shape
    k4 = k.reshape(b, n_chunks, CHUNK, DIFF_WIDTH)
    return pl.pallas_call(
        functools.partial(_df_kernel, n_chunks=n_chunks),
        grid=(b, DIFF_HEADS, n_chunks),
        in_specs=[
            pl.BlockSpec((1, 1, 128, CHUNK), lambda bi, h, qi: (bi, qi, h, 0)),
            pl.BlockSpec((1, n_chunks, CHUNK, 128), lambda bi, h, qi: (bi, 0, 0, h)),
            pl.BlockSpec((1, n_chunks, 128, CHUNK), lambda bi, h, qi: (bi, 0, h, 0)),
            pl.BlockSpec((N_META, 128), lambda bi, h, qi: (0, h)),
            pl.BlockSpec((128, N_META), lambda bi, h, qi: (h, 0)),
            pl.BlockSpec((4, DIFF_HEAD_DIM), lambda bi, h, qi: (0, 0)),
            pl.BlockSpec((128, 1), lambda bi, h, qi: (0, 0)),
        ],
        out_specs=pl.BlockSpec((1, 1, 128, CHUNK), lambda bi, h, qi: (bi, qi, h, 0)),
        out_shape=jax.ShapeDtypeStruct(qt.shape, BF16),
        compiler_params=pltpu.CompilerParams(
            dimension_semantics=("parallel", "parallel", "parallel"),
            vmem_limit_bytes=VMEM_LIMIT_BYTES),
        name="df_attn",
    )(qt, k4, vt, k_meta, vt_meta, lam_vecs, subln_col)


def _out_ffn_kernel(x_ref, nat_ref, dft_ref, g_na_ref, g_df_ref, w_na_ref, w_df_ref, w_o_ref,
                    ffn_g_ref, w_gate_ref, w_up_ref, w_down_ref, fin_g_ref, o_ref):
    o_na = _dot_tn(nat_ref[0, 0], w_na_ref[...])
    o_df = _dot_tn(dft_ref[0, 0], w_df_ref[...])
    merged = g_na_ref[0].astype(F32) * o_na + g_df_ref[0].astype(F32) * o_df
    x1 = x_ref[0] + _dot(merged.astype(BF16), w_o_ref[...])
    h = (_rms_scale(x1, NORM_EPS) * ffn_g_ref[...]).astype(BF16)
    gate = _dot(h, w_gate_ref[...])
    up = _dot(h, w_up_ref[...])
    act = (gate * jax.nn.sigmoid(gate) * up).astype(BF16)
    x2 = x1 + _dot(act, w_down_ref[...])
    o_ref[0] = _rms_scale(x2, NORM_EPS) * fin_g_ref[...]


def _out_ffn(x, nat, dft, g_na, g_df, w_na, w_df, w_o, ffn_g, w_gate, w_up, w_down, fin_g, rows):
    b, t, _ = x.shape
    per_chunk = CHUNK // rows
    const = lambda shape: pl.BlockSpec(shape, lambda i, j: (0,) * len(shape),
                                       pipeline_mode=pl.Buffered(1))
    tok_spec = pl.BlockSpec((1, rows, D_MODEL), lambda i, j: (i, j, 0))
    feat_spec = pl.BlockSpec((1, 1, 512, rows), lambda i, j: (i, j // per_chunk, 0, j % per_chunk))
    return pl.pallas_call(
        _out_ffn_kernel,
        grid=(b, t // rows),
        in_specs=[tok_spec, feat_spec, feat_spec, tok_spec, tok_spec,
                  const((NA_WIDTH, D_MODEL)), const((DIFF_WIDTH, D_MODEL)), const((D_MODEL, D_MODEL)),
                  const((1, D_MODEL)), const((D_MODEL, D_FF)), const((D_MODEL, D_FF)),
                  const((D_FF, D_MODEL)), const((1, D_MODEL))],
        out_specs=tok_spec,
        out_shape=jax.ShapeDtypeStruct(x.shape, F32),
        compiler_params=pltpu.CompilerParams(
            dimension_semantics=("parallel", "parallel"), vmem_limit_bytes=VMEM_LIMIT_BYTES),
        name="out_ffn",
    )(x, nat, dft, g_na, g_df, w_na, w_df, w_o, ffn_g, w_gate, w_up, w_down, fin_g)


def kernel(x, meta_tokens, mix_norm, w_in, na_rpb, lambda_q1, lambda_k1, lambda_q2, lambda_k2,
           diff_subln, w_na_out, w_diff_out, w_o, ffn_norm, w_gate, w_up, w_down, final_norm):
    b, t, _ = x.shape
    assert mix_norm.shape[0] == 1, "single-layer block"
    assert t % CHUNK == 0 and t // CHUNK >= 3

    w = w_in[0].astype(BF16)
    na_q, na_k, na_v, df_q, df_k, df_v, g_na, g_df = jnp.split(
        w, [512, 1024, 1536, 2048, 2560, 3072, 4096], axis=1)
    w_t = jnp.concatenate([na_q, na_v, df_q, df_v], axis=1).T
    w_s = jnp.concatenate([na_k, df_k, g_na, g_df], axis=1)
    gain = mix_norm[0][None].astype(F32)

    pos = jnp.arange(N_META + t, dtype=jnp.int32)
    tables_meta = _rope_tables(pos[:N_META])
    tables_real = _rope_tables(pos[N_META:])

    na_qt, na_vt, df_qt, df_vt, na_k_r, df_k_r, g_na_r, g_df_r = _in_proj(
        x, gain, w_t, w_s, *tables_real, rows=CHUNK)
    meta = _in_proj(meta_tokens[None].astype(x.dtype), gain, w_t, w_s, *tables_meta, rows=N_META)
    na_vt_m, df_vt_m, na_k_m, df_k_m = meta[1][0, 0], meta[3][0, 0], meta[4][0], meta[5][0]

    bias = _na_bias_tiles(na_rpb[0])
    na_out_t = _na_attention(na_qt, na_k_r, na_vt, na_k_m, na_vt_m, bias)

    lam_vecs = jnp.stack([lambda_q1[0], lambda_k1[0], lambda_q2[0], lambda_k2[0]]).astype(F32)
    df_out_t = _df_attention(df_qt, df_k_r, df_vt, df_k_m, df_vt_m, lam_vecs,
                             diff_subln[0].astype(F32)[:, None])

    return _out_ffn(x, na_out_t, df_out_t, g_na_r, g_df_r,
                    w_na_out[0].astype(BF16), w_diff_out[0].astype(BF16), w_o[0].astype(BF16),
                    ffn_norm[0][None].astype(F32), w_gate[0].astype(BF16), w_up[0].astype(BF16),
                    w_down[0].astype(BF16), final_norm[None].astype(F32), rows=256)
```

```python
import functools
import math

import jax
import jax.numpy as jnp
from jax import lax
from jax.experimental import pallas as pl
from jax.experimental.pallas import tpu as pltpu

D_MODEL = 1024
GRID_W = 64
N_META = 16
NA_HEADS = 8
NA_HEAD_DIM = 64
NA_WIN_ROWS = 8
NA_WIN_COLS = 16
DIFF_HEADS = 4
DIFF_HEAD_DIM = 64
NA_WIDTH = NA_HEADS * NA_HEAD_DIM
DIFF_WIDTH = DIFF_HEADS * 2 * DIFF_HEAD_DIM
D_FF = 2816
ROPE_THETA = 10000.0
NORM_EPS = 1e-6
SUBLN_EPS = 1e-5
LAMBDA_INIT = 0.8 - 0.6 * math.exp(-0.3 * 0)

CHUNK = 512
NA_GROUP_ROWS = CHUNK // GRID_W
NA_KEY_ROWS = 16
NA_KEY_BLOCK = 256
DF_SUB = 512
LOG2_E = math.log2(math.e)
MASK_VALUE = -1e30
VMEM_LIMIT_BYTES = 56 * 1024 * 1024

BF16 = jnp.bfloat16
F32 = jnp.float32


def _dot(a, b):
    return jnp.dot(a, b, preferred_element_type=F32)


def _dot_nt(a, b):
    return lax.dot_general(a, b, (((1,), (1,)), ((), ())), preferred_element_type=F32)


def _dot_tn(a, b):
    return lax.dot_general(a, b, (((0,), (0,)), ((), ())), preferred_element_type=F32)


def _rms_scale(x, eps):
    return x * lax.rsqrt(jnp.mean(x * x, axis=-1, keepdims=True) + eps)


def _in_proj_kernel(x_ref, g_ref, wt_ref, ws_ref, cos_t_ref, sin_t_ref, cos_s_ref, sin_s_ref,
                    na_qt_ref, na_vt_ref, df_qt_ref, df_vt_ref, na_k_ref, df_k_ref,
                    g_na_ref, g_df_ref):
    h = (_rms_scale(x_ref[0], NORM_EPS) * g_ref[...]).astype(BF16)

    na_qt_ref[0, 0] = (_dot_nt(wt_ref[0:512], h) * (NA_HEAD_DIM ** -0.5)).astype(BF16)
    na_vt_ref[0, 0] = _dot_nt(wt_ref[512:1024], h).astype(BF16)
    df_vt_ref[0, 0] = _dot_nt(wt_ref[1536:2048], h).astype(BF16)
    dq = _dot_nt(wt_ref[1024:1536], h) * (LOG2_E * DIFF_HEAD_DIM ** -0.5)
    cos_t = cos_t_ref[...]
    sin_t = sin_t_ref[...]
    half = DIFF_HEAD_DIM // 2
    for c in range(DIFF_WIDTH // DIFF_HEAD_DIM):
        x1 = dq[c * 64:c * 64 + half]
        x2 = dq[c * 64 + half:(c + 1) * 64]
        df_qt_ref[0, 0, c * 64:c * 64 + half, :] = (x1 * cos_t - x2 * sin_t).astype(BF16)
        df_qt_ref[0, 0, c * 64 + half:(c + 1) * 64, :] = (x2 * cos_t + x1 * sin_t).astype(BF16)

    na_k_ref[0] = _dot(h, ws_ref[:, 0:512]).astype(BF16)
    dk = _dot(h, ws_ref[:, 512:1024])
    cos_s = cos_s_ref[...]
    sin_s = sin_s_ref[...]
    lane = lax.broadcasted_iota(jnp.int32, cos_s.shape, 1)
    low_half = (lane % DIFF_HEAD_DIM) < half
    for c in range(DIFF_WIDTH // 128):
        xk = dk[:, c * 128:(c + 1) * 128]
        partner = jnp.where(low_half, pltpu.roll(xk, 128 - half, 1), pltpu.roll(xk, half, 1))
        df_k_ref[0, :, c * 128:(c + 1) * 128] = (xk * cos_s + partner * sin_s).astype(BF16)
    g_na_ref[0] = jax.nn.sigmoid(_dot(h, ws_ref[:, 1024:2048])).astype(BF16)
    g_df_ref[0] = jax.nn.sigmoid(_dot(h, ws_ref[:, 2048:3072])).astype(BF16)


def _in_proj(x, gain, w_t, w_s, cos_t, sin_t, cos_s, sin_s, rows):
    b, t, _ = x.shape
    n = t // rows
    const = lambda shape: pl.BlockSpec(shape, lambda i, j: (0,) * len(shape),
                                       pipeline_mode=pl.Buffered(1))
    feat_major = jax.ShapeDtypeStruct((b, n, 512, rows), BF16)
    feat_spec = pl.BlockSpec((1, 1, 512, rows), lambda i, j: (i, j, 0, 0))
    tok = lambda width: jax.ShapeDtypeStruct((b, t, width), BF16)
    tok_spec = lambda width: pl.BlockSpec((1, rows, width), lambda i, j: (i, j, 0))
    return pl.pallas_call(
        _in_proj_kernel,
        grid=(b, n),
        in_specs=[
            pl.BlockSpec((1, rows, D_MODEL), lambda i, j: (i, j, 0)),
            const((1, D_MODEL)),
            const((2048, D_MODEL)),
            const((D_MODEL, 3072)),
            pl.BlockSpec((DIFF_HEAD_DIM // 2, rows), lambda i, j: (0, j)),
            pl.BlockSpec((DIFF_HEAD_DIM // 2, rows), lambda i, j: (0, j)),
            pl.BlockSpec((rows, 128), lambda i, j: (j, 0)),
            pl.BlockSpec((rows, 128), lambda i, j: (j, 0)),
        ],
        out_specs=[feat_spec, feat_spec, feat_spec, feat_spec,
                   tok_spec(512), tok_spec(512), tok_spec(D_MODEL), tok_spec(D_MODEL)],
        out_shape=[feat_major, feat_major, feat_major, feat_major,
                   tok(512), tok(512), tok(D_MODEL), tok(D_MODEL)],
        compiler_params=pltpu.CompilerParams(
            dimension_semantics=("parallel", "parallel"), vmem_limit_bytes=VMEM_LIMIT_BYTES),
        name="in_proj",
    )(x, gain, w_t, w_s, cos_t, sin_t, cos_s, sin_s)


def _rope_tables(pos):
    half = DIFF_HEAD_DIM // 2
    inv = ROPE_THETA ** (-jnp.arange(half, dtype=F32) / half)
    ang = pos.astype(F32)[:, None] * inv[None, :]
    cos, sin = jnp.cos(ang), jnp.sin(ang)
    cos_s = jnp.tile(cos, (1, 128 // half))
    sin_s = jnp.tile(jnp.concatenate([-sin, sin], axis=1), (1, 128 // DIFF_HEAD_DIM))
    return cos.T, sin.T, cos_s, sin_s


def _na_bias_tiles(rpb):
    w = jnp.arange(GRID_W)
    c = jnp.arange(GRID_W)
    col_start = jnp.clip(w - NA_WIN_COLS // 2, 0, GRID_W - NA_WIN_COLS)
    col_ok = (c[:, None] >= col_start[None, :]) & (c[:, None] < col_start[None, :] + NA_WIN_COLS)
    col_idx = jnp.clip(c[:, None] - w[None, :] + NA_WIN_COLS - 1, 0, 2 * NA_WIN_COLS - 2)
    per_dr = rpb.astype(F32)[:, :, col_idx]
    per_dr = jnp.where(col_ok[None, None], per_dr, MASK_VALUE)
    masked = jnp.full((rpb.shape[0], GRID_W, GRID_W), MASK_VALUE, F32)

    def tile(dr_of, ok_of):
        cols = []
        for j in range(NA_GROUP_ROWS):
            blocks = [per_dr[:, dr_of(i, j) + NA_WIN_ROWS - 1] if ok_of(i, j) else masked
                      for i in range(NA_KEY_ROWS)]
            cols.append(jnp.concatenate(blocks, axis=1))
        return jnp.concatenate(cols, axis=2)

    first = tile(lambda i, j: i - j, lambda i, j: max(j - 4, 0) <= i < max(j - 4, 0) + NA_WIN_ROWS)
    inner = tile(lambda i, j: i - j - 4, lambda i, j: 0 <= i - j < NA_WIN_ROWS)
    last = tile(lambda i, j: i - j - 8,
                lambda i, j: min(j + 4, 8) <= i < min(j + 4, 8) + NA_WIN_ROWS)
    return jnp.stack([first, inner, last])


def _na_kernel(qt_ref, k0_ref, k1_ref, k2_ref, k3_ref, v0_ref, v1_ref, v2_ref, v3_ref,
               km_ref, vm_ref, bias_ref, o_ref):
    qt = qt_ref[0, 0]
    k_all = jnp.concatenate([k0_ref[0], k1_ref[0], k2_ref[0], k3_ref[0]], axis=0)
    vt_all = jnp.concatenate([v0_ref[0, 0], v1_ref[0, 0], v2_ref[0, 0], v3_ref[0, 0]], axis=1)
    k_meta = km_ref[...]
    vt_meta = vm_ref[...]
    row = lax.broadcasted_iota(jnp.int32, qt.shape, 0)
    outs = []
    for hh in range(2):
        in_head = (row >= hh * NA_HEAD_DIM) & (row < (hh + 1) * NA_HEAD_DIM)
        qt_h = jnp.where(in_head, qt, jnp.zeros_like(qt))
        s = _dot(k_all, qt_h) + bias_ref[0, hh]
        s_meta = _dot(k_meta, qt_h)
        m = jnp.maximum(jnp.max(s, axis=0, keepdims=True), jnp.max(s_meta, axis=0, keepdims=True))
        p = jnp.exp(s - m)
        p_meta = jnp.exp(s_meta - m)
        l = jnp.sum(p, axis=0, keepdims=True) + jnp.sum(p_meta, axis=0, keepdims=True)
        o = _dot(vt_all, p.astype(BF16)) + _dot(vt_meta, p_meta.astype(BF16))
        outs.append(o / l)
    o_ref[0, 0] = jnp.where(row < NA_HEAD_DIM, outs[0], outs[1]).astype(BF16)


def _na_attention(qt, k, vt, k_meta, vt_meta, bias):
    b, n_groups, _, _ = qt.shape
    n_kblocks = n_groups * (CHUNK // NA_KEY_BLOCK)
    per_chunk = CHUNK // NA_KEY_BLOCK

    def kblock(g, i):
        return jnp.clip(2 * g - 1, 0, n_kblocks - NA_KEY_ROWS * GRID_W // NA_KEY_BLOCK) + i

    def variant(g):
        return jnp.where(g == 0, 0, jnp.where(g == n_groups - 1, 2, 1))

    k_specs = [pl.BlockSpec((1, NA_KEY_BLOCK, 128),
                            functools.partial(lambda i, hp, g, bi: (bi, kblock(g, i), hp), i))
               for i in range(4)]
    v_specs = [pl.BlockSpec((1, 1, 128, NA_KEY_BLOCK),
                            functools.partial(
                                lambda i, hp, g, bi: (bi, kblock(g, i) // per_chunk, hp,
                                                      kblock(g, i) % per_chunk), i))
               for i in range(4)]
    return pl.pallas_call(
        _na_kernel,
        grid=(NA_HEADS // 2, n_groups, b),
        in_specs=[pl.BlockSpec((1, 1, 128, CHUNK), lambda hp, g, bi: (bi, g, hp, 0))]
        + k_specs + v_specs + [
            pl.BlockSpec((N_META, 128), lambda hp, g, bi: (0, hp)),
            pl.BlockSpec((128, N_META), lambda hp, g, bi: (hp, 0)),
            pl.BlockSpec((1, 2, NA_KEY_ROWS * GRID_W, CHUNK),
                         lambda hp, g, bi: (variant(g), hp, 0, 0)),
        ],
        out_specs=pl.BlockSpec((1, 1, 128, CHUNK), lambda hp, g, bi: (bi, g, hp, 0)),
        out_shape=jax.ShapeDtypeStruct(qt.shape, BF16),
        compiler_params=pltpu.CompilerParams(
            dimension_semantics=("parallel", "parallel", "parallel"),
            vmem_limit_bytes=VMEM_LIMIT_BYTES),
        name="na_attn",
    )(qt, k, k, k, k, vt, vt, vt, vt, k_meta, vt_meta, bias)


def _col_part(x, op):
    return op(x.reshape(x.shape[0] // 8, 8, x.shape[1]), axis=0)


def _df_kernel(qt_ref, k_ref, vt_ref, km_ref, vm_ref, lam_ref, g_ref, o_ref,
               s_a0, s_a1, s_b0, s_b1, acc0_ref, acc1_ref, *, n_chunks):
    qt = qt_ref[0, 0]
    row = lax.broadcasted_iota(jnp.int32, qt.shape, 0)
    zero = jnp.zeros_like(qt)
    qts = (jnp.where(row < DIFF_HEAD_DIM, qt, zero), jnp.where(row >= DIFF_HEAD_DIM, qt, zero))
    n_sub = CHUNK // DF_SUB
    s_even, s_odd = (s_a0, s_a1), (s_b0, s_b1)
    acc_refs = (acc0_ref, acc1_ref)

    def stage(i, s_cur, s_nxt, m, alpha, l8, with_next):
        part_max = [None, None]
        pv = [None, None]
        l8 = list(l8)
        for j in range(n_sub):
            rows = slice(j * DF_SUB, (j + 1) * DF_SUB)
            for c in range(2):
                if with_next:
                    s_next = _dot(k_ref[0, i + 1, rows, :], qts[c])
                    s_nxt[c][rows, :] = s_next
                    part = _col_part(s_next, jnp.max)
                    part_max[c] = part if j == 0 else jnp.maximum(part_max[c], part)
                p = jnp.exp2(s_cur[c][rows, :] - m[c])
                l8[c] = l8[c] + _col_part(p, jnp.sum)
                d = _dot(vt_ref[0, i, :, rows], p.astype(BF16))
                pv[c] = d if j == 0 else pv[c] + d
        m_new, alpha_new = [], []
        for c in range(2):
            acc_refs[c][...] = acc_refs[c][...] * alpha[c] + pv[c]
            if with_next:
                mn = jnp.maximum(m[c], jnp.max(part_max[c], axis=0, keepdims=True))
                an = jnp.exp2(m[c] - mn)
                l8[c] = l8[c] * an
                m_new.append(mn)
                alpha_new.append(an)
        return tuple(m_new), tuple(alpha_new), tuple(l8)

    k_meta = km_ref[...]
    vt_meta = vm_ref[...]
    m, alpha, l8 = [], [], []
    for c in range(2):
        s_meta = _dot(k_meta, qts[c])
        s0 = _dot(k_ref[0, 0], qts[c])
        s_even[c][...] = s0
        mc = jnp.maximum(jnp.max(s_meta, axis=0, keepdims=True), jnp.max(s0, axis=0, keepdims=True))
        p_meta = jnp.exp2(s_meta - mc)
        acc_refs[c][...] = _dot(vt_meta, p_meta.astype(BF16))
        m.append(mc)
        alpha.append(jnp.ones_like(mc))
        l8.append(_col_part(p_meta, jnp.sum))

    def body(i2, carry):
        carry = stage(2 * i2, s_even, s_odd, *carry, True)
        return stage(2 * i2 + 1, s_odd, s_even, *carry, True)

    assert n_chunks % 2 == 0
    carry = lax.fori_loop(0, n_chunks // 2 - 1, body, (tuple(m), tuple(alpha), tuple(l8)))
    carry = stage(n_chunks - 2, s_even, s_odd, *carry, True)
    _, _, l8 = stage(n_chunks - 1, s_odd, s_even, *carry, False)

    lam_v = lam_ref[...]
    lam = (jnp.exp(jnp.sum(lam_v[0:1] * lam_v[1:2], axis=1, keepdims=True))
           - jnp.exp(jnp.sum(lam_v[2:3] * lam_v[3:4], axis=1, keepdims=True)) + LAMBDA_INIT)
    l0 = jnp.sum(l8[0], axis=0, keepdims=True)
    l1 = jnp.sum(l8[1], axis=0, keepdims=True)
    o = acc0_ref[...] / l0 - lam * (acc1_ref[...] / l1)
    o = o * lax.rsqrt(jnp.mean(o * o, axis=0, keepdims=True) + SUBLN_EPS)
    o_ref[0, 0] = (o * g_ref[...] * (1.0 - LAMBDA_INIT)).astype(BF16)


def _df_attention(qt, k, vt, k_meta, vt_meta, lam_vecs, subln_col):
    b, n_chunks, _, _ = qt.shape
    k4 = k.reshape(b, n_chunks, CHUNK, DIFF_WIDTH)
    return pl.pallas_call(
        functools.partial(_df_kernel, n_chunks=n_chunks),
        grid=(b, DIFF_HEADS, n_chunks),
        in_specs=[
            pl.BlockSpec((1, 1, 128, CHUNK), lambda bi, h, qi: (bi, qi, h, 0)),
            pl.BlockSpec((1, n_chunks, CHUNK, 128), lambda bi, h, qi: (bi, 0, 0, h)),
            pl.BlockSpec((1, n_chunks, 128, CHUNK), lambda bi, h, qi: (bi, 0, h, 0)),
            pl.BlockSpec((N_META, 128), lambda bi, h, qi: (0, h)),
            pl.BlockSpec((128, N_META), lambda bi, h, qi: (h, 0)),
            pl.BlockSpec((4, DIFF_HEAD_DIM), lambda bi, h, qi: (0, 0)),
            pl.BlockSpec((128, 1), lambda bi, h, qi: (0, 0)),
        ],
        out_specs=pl.BlockSpec((1, 1, 128, CHUNK), lambda bi, h, qi: (bi, qi, h, 0)),
        out_shape=jax.ShapeDtypeStruct(qt.shape, BF16),
        scratch_shapes=[pltpu.VMEM((CHUNK, CHUNK), F32)] * 4
        + [pltpu.VMEM((128, CHUNK), F32)] * 2,
        compiler_params=pltpu.CompilerParams(
            dimension_semantics=("parallel", "parallel", "parallel"),
            vmem_limit_bytes=VMEM_LIMIT_BYTES),
        name="df_attn",
    )(qt, k4, vt, k_meta, vt_meta, lam_vecs, subln_col)


def _out_ffn_kernel(x_ref, nat_ref, dft_ref, g_na_ref, g_df_ref, w_na_ref, w_df_ref, w_o_ref,
                    ffn_g_ref, w_gate_ref, w_up_ref, w_down_ref, fin_g_ref, o_ref):
    o_na = _dot_tn(nat_ref[0, 0], w_na_ref[...])
    o_df = _dot_tn(dft_ref[0, 0], w_df_ref[...])
    merged = g_na_ref[0].astype(F32) * o_na + g_df_ref[0].astype(F32) * o_df
    x1 = x_ref[0] + _dot(merged.astype(BF16), w_o_ref[...])
    h = (_rms_scale(x1, NORM_EPS) * ffn_g_ref[...]).astype(BF16)
    gate = _dot(h, w_gate_ref[...])
    up = _dot(h, w_up_ref[...])
    act = (gate * jax.nn.sigmoid(gate) * up).astype(BF16)
    x2 = x1 + _dot(act, w_down_ref[...])
    o_ref[0] = _rms_scale(x2, NORM_EPS) * fin_g_ref[...]


def _out_ffn(x, nat, dft, g_na, g_df, w_na, w_df, w_o, ffn_g, w_gate, w_up, w_down, fin_g, rows):
    b, t, _ = x.shape
    per_chunk = CHUNK // rows
    const = lambda shape: pl.BlockSpec(shape, lambda i, j: (0,) * len(shape),
                                       pipeline_mode=pl.Buffered(1))
    tok_spec = pl.BlockSpec((1, rows, D_MODEL), lambda i, j: (i, j, 0))
    feat_spec = pl.BlockSpec((1, 1, 512, rows), lambda i, j: (i, j // per_chunk, 0, j % per_chunk))
    return pl.pallas_call(
        _out_ffn_kernel,
        grid=(b, t // rows),
        in_specs=[tok_spec, feat_spec, feat_spec, tok_spec, tok_spec,
                  const((NA_WIDTH, D_MODEL)), const((DIFF_WIDTH, D_MODEL)), const((D_MODEL, D_MODEL)),
                  const((1, D_MODEL)), const((D_MODEL, D_FF)), const((D_MODEL, D_FF)),
                  const((D_FF, D_MODEL)), const((1, D_MODEL))],
        out_specs=tok_spec,
        out_shape=jax.ShapeDtypeStruct(x.shape, F32),
        compiler_params=pltpu.CompilerParams(
            dimension_semantics=("parallel", "parallel"), vmem_limit_bytes=VMEM_LIMIT_BYTES),
        name="out_ffn",
    )(x, nat, dft, g_na, g_df, w_na, w_df, w_o, ffn_g, w_gate, w_up, w_down, fin_g)


def kernel(x, meta_tokens, mix_norm, w_in, na_rpb, lambda_q1, lambda_k1, lambda_q2, lambda_k2,
           diff_subln, w_na_out, w_diff_out, w_o, ffn_norm, w_gate, w_up, w_down, final_norm):
    b, t, _ = x.shape
    assert mix_norm.shape[0] == 1, "single-layer block"
    assert t % CHUNK == 0 and t // CHUNK >= 3

    w = w_in[0].astype(BF16)
    na_q, na_k, na_v, df_q, df_k, df_v, g_na, g_df = jnp.split(
        w, [512, 1024, 1536, 2048, 2560, 3072, 4096], axis=1)
    w_t = jnp.concatenate([na_q, na_v, df_q, df_v], axis=1).T
    w_s = jnp.concatenate([na_k, df_k, g_na, g_df], axis=1)
    gain = mix_norm[0][None].astype(F32)

    pos = jnp.arange(N_META + t, dtype=jnp.int32)
    tables_meta = _rope_tables(pos[:N_META])
    tables_real = _rope_tables(pos[N_META:])

    na_qt, na_vt, df_qt, df_vt, na_k_r, df_k_r, g_na_r, g_df_r = _in_proj(
        x, gain, w_t, w_s, *tables_real, rows=CHUNK)
    meta = _in_proj(meta_tokens[None].astype(x.dtype), gain, w_t, w_s, *tables_meta, rows=N_META)
    na_vt_m, df_vt_m, na_k_m, df_k_m = meta[1][0, 0], meta[3][0, 0], meta[4][0], meta[5][0]

    bias = _na_bias_tiles(na_rpb[0])
    na_out_t = _na_attention(na_qt, na_k_r, na_vt, na_k_m, na_vt_m, bias)

    lam_vecs = jnp.stack([lambda_q1[0], lambda_k1[0], lambda_q2[0], lambda_k2[0]]).astype(F32)
    df_out_t = _df_attention(df_qt, df_k_r, df_vt, df_k_m, df_vt_m, lam_vecs,
                             diff_subln[0].astype(F32)[:, None])

    return _out_ffn(x, na_out_t, df_out_t, g_na_r, g_df_r,
                    w_na_out[0].astype(BF16), w_diff_out[0].astype(BF16), w_o[0].astype(BF16),
                    ffn_norm[0][None].astype(F32), w_gate[0].astype(BF16), w_up[0].astype(BF16),
                    w_down[0].astype(BF16), final_norm[None].astype(F32), rows=256)
```

```python
import functools
import math

import jax
import jax.numpy as jnp
from jax import lax
from jax.experimental import pallas as pl
from jax.experimental.pallas import tpu as pltpu

D_MODEL = 1024
GRID_W = 64
N_META = 16
NA_HEADS = 8
NA_HEAD_DIM = 64
NA_WIN_ROWS = 8
NA_WIN_COLS = 16
DIFF_HEADS = 4
DIFF_HEAD_DIM = 64
NA_WIDTH = NA_HEADS * NA_HEAD_DIM
DIFF_WIDTH = DIFF_HEADS * 2 * DIFF_HEAD_DIM
D_FF = 2816
ROPE_THETA = 10000.0
NORM_EPS = 1e-6
SUBLN_EPS = 1e-5
LAMBDA_INIT = 0.8 - 0.6 * math.exp(-0.3 * 0)

CHUNK = 512
NA_GROUP_ROWS = CHUNK // GRID_W
NA_KEY_ROWS = 16
NA_KEY_BLOCK = 256
V_ROWS = 128 + 16
LOG2_E = math.log2(math.e)
MASK_VALUE = -1e30
VMEM_LIMIT_BYTES = 56 * 1024 * 1024

BF16 = jnp.bfloat16
F32 = jnp.float32


def _dot(a, b):
    return jnp.dot(a, b, preferred_element_type=F32)


def _dot_nt(a, b):
    return lax.dot_general(a, b, (((1,), (1,)), ((), ())), preferred_element_type=F32)


def _dot_tn(a, b):
    return lax.dot_general(a, b, (((0,), (0,)), ((), ())), preferred_element_type=F32)


def _rms_scale(x, eps):
    return x * lax.rsqrt(jnp.mean(x * x, axis=-1, keepdims=True) + eps)


def _in_proj_kernel(x_ref, g_ref, wt_ref, ws_ref, cos_t_ref, sin_t_ref, cos_s_ref, sin_s_ref,
                    na_qt_ref, na_vt_ref, df_qt_ref, df_vt_ref, na_k_ref, df_k_ref,
                    g_na_ref, g_df_ref):
    h = (_rms_scale(x_ref[0], NORM_EPS) * g_ref[...]).astype(BF16)

    na_qt_ref[0, 0] = (_dot_nt(wt_ref[0:512], h) * (LOG2_E * NA_HEAD_DIM ** -0.5)).astype(BF16)
    nv = _dot_nt(wt_ref[512:1024], h).astype(BF16)
    dv = _dot_nt(wt_ref[1536:2048], h).astype(BF16)
    pad_row = lax.broadcasted_iota(jnp.int32, (V_ROWS - 128, dv.shape[1]), 0)
    ones_pad = jnp.where(pad_row == 0, 1.0, 0.0).astype(BF16)
    for hd in range(DIFF_HEADS):
        na_vt_ref[0, hd, 0:128, :] = nv[hd * 128:(hd + 1) * 128]
        na_vt_ref[0, hd, 128:V_ROWS, :] = ones_pad
        df_vt_ref[0, 0, hd, 0:128, :] = dv[hd * 128:(hd + 1) * 128]
        df_vt_ref[0, 0, hd, 128:V_ROWS, :] = ones_pad
    dq = _dot_nt(wt_ref[1024:1536], h) * (LOG2_E * DIFF_HEAD_DIM ** -0.5)
    cos_t = cos_t_ref[...]
    sin_t = sin_t_ref[...]
    half = DIFF_HEAD_DIM // 2
    for c in range(DIFF_WIDTH // DIFF_HEAD_DIM):
        x1 = dq[c * 64:c * 64 + half]
        x2 = dq[c * 64 + half:(c + 1) * 64]
        df_qt_ref[0, 0, c * 64:c * 64 + half, :] = (x1 * cos_t - x2 * sin_t).astype(BF16)
        df_qt_ref[0, 0, c * 64 + half:(c + 1) * 64, :] = (x2 * cos_t + x1 * sin_t).astype(BF16)

    na_k_ref[0] = _dot(h, ws_ref[:, 0:512]).astype(BF16)
    dk = _dot(h, ws_ref[:, 512:1024])
    cos_s = cos_s_ref[...]
    sin_s = sin_s_ref[...]
    lane = lax.broadcasted_iota(jnp.int32, cos_s.shape, 1)
    low_half = (lane % DIFF_HEAD_DIM) < half
    for c in range(DIFF_WIDTH // 128):
        xk = dk[:, c * 128:(c + 1) * 128]
        partner = jnp.where(low_half, pltpu.roll(xk, 128 - half, 1), pltpu.roll(xk, half, 1))
        df_k_ref[0, :, c * 128:(c + 1) * 128] = (xk * cos_s + partner * sin_s).astype(BF16)
    g_na_ref[0] = jax.nn.sigmoid(_dot(h, ws_ref[:, 1024:2048])).astype(BF16)
    g_df_ref[0] = jax.nn.sigmoid(_dot(h, ws_ref[:, 2048:3072])).astype(BF16)


def _in_proj(x, gain, w_t, w_s, cos_t, sin_t, cos_s, sin_s, rows):
    b, t, _ = x.shape
    n = t // rows
    const = lambda shape: pl.BlockSpec(shape, lambda i, j: (0,) * len(shape),
                                       pipeline_mode=pl.Buffered(1))
    feat_major = jax.ShapeDtypeStruct((b, n, 512, rows), BF16)
    feat_spec = pl.BlockSpec((1, 1, 512, rows), lambda i, j: (i, j, 0, 0))
    value_major = jax.ShapeDtypeStruct((b, n, 4, V_ROWS, rows), BF16)
    value_spec = pl.BlockSpec((1, 1, 4, V_ROWS, rows), lambda i, j: (i, j, 0, 0, 0))
    tok = lambda width: jax.ShapeDtypeStruct((b, t, width), BF16)
    tok_spec = lambda width: pl.BlockSpec((1, rows, width), lambda i, j: (i, j, 0))
    return pl.pallas_call(
        _in_proj_kernel,
        grid=(b, n),
        in_specs=[
            pl.BlockSpec((1, rows, D_MODEL), lambda i, j: (i, j, 0)),
            const((1, D_MODEL)),
            const((2048, D_MODEL)),
            const((D_MODEL, 3072)),
            pl.BlockSpec((DIFF_HEAD_DIM // 2, rows), lambda i, j: (0, j)),
            pl.BlockSpec((DIFF_HEAD_DIM // 2, rows), lambda i, j: (0, j)),
            pl.BlockSpec((rows, 128), lambda i, j: (j, 0)),
            pl.BlockSpec((rows, 128), lambda i, j: (j, 0)),
        ],
        out_specs=[feat_spec,
                   pl.BlockSpec((1, 4, V_ROWS, rows), lambda i, j: (i, 0, 0, j)),
                   feat_spec, value_spec,
                   tok_spec(512), tok_spec(512), tok_spec(D_MODEL), tok_spec(D_MODEL)],
        out_shape=[feat_major, jax.ShapeDtypeStruct((b, 4, V_ROWS, t), BF16), feat_major, value_major,
                   tok(512), tok(512), tok(D_MODEL), tok(D_MODEL)],
        compiler_params=pltpu.CompilerParams(
            dimension_semantics=("parallel", "parallel"), vmem_limit_bytes=VMEM_LIMIT_BYTES),
        name="in_proj",
    )(x, gain, w_t, w_s, cos_t, sin_t, cos_s, sin_s)


def _rope_tables(pos):
    half = DIFF_HEAD_DIM // 2
    inv = ROPE_THETA ** (-jnp.arange(half, dtype=F32) / half)
    ang = pos.astype(F32)[:, None] * inv[None, :]
    cos, sin = jnp.cos(ang), jnp.sin(ang)
    cos_s = jnp.tile(cos, (1, 128 // half))
    sin_s = jnp.tile(jnp.concatenate([-sin, sin], axis=1), (1, 128 // DIFF_HEAD_DIM))
    return cos.T, sin.T, cos_s, sin_s


def _na_bias_tiles(rpb):
    n_heads = rpb.shape[0]
    reach = GRID_W - NA_WIN_COLS
    padded = jnp.pad(rpb.astype(F32) * LOG2_E, ((0, 0), (0, 0), (reach, reach)))
    table = jnp.stack([padded[:, :, GRID_W - 1 - w:2 * GRID_W - 1 - w]
                       for w in range(GRID_W)], axis=-1)
    w = jnp.arange(GRID_W)
    c = jnp.arange(GRID_W)
    col_start = jnp.clip(w - NA_WIN_COLS // 2, 0, GRID_W - NA_WIN_COLS)
    col_ok = (c[:, None] >= col_start[None, :]) & (c[:, None] < col_start[None, :] + NA_WIN_COLS)
    table = jnp.where(col_ok[None, None], table, MASK_VALUE)
    table = jnp.concatenate([table, table], axis=-1)

    def first(i, j):
        return i - j if max(j - 4, 0) <= i < max(j - 4, 0) + NA_WIN_ROWS else None

    def inner(i, j):
        return i - j - 4 if 0 <= i - j < NA_WIN_ROWS else None

    def last(i, j):
        return i - j - 8 if min(j + 4, 8) <= i < min(j + 4, 8) + NA_WIN_ROWS else None

    def body(tab_ref, o_ref):
        lane = lax.broadcasted_iota(jnp.int32, (GRID_W, 128), 1)
        masked = jnp.full((GRID_W, 128), MASK_VALUE, F32)
        for v, offset in enumerate((first, inner, last)):
            for i in range(NA_KEY_ROWS):
                for jp in range(NA_GROUP_ROWS // 2):
                    halves = []
                    for j in (2 * jp, 2 * jp + 1):
                        dr = offset(i, j)
                        halves.append(masked if dr is None else tab_ref[0, dr + NA_WIN_ROWS - 1])
                    o_ref[v, 0, i * GRID_W:(i + 1) * GRID_W, jp * 128:(jp + 1) * 128] = jnp.where(
                        lane < GRID_W, halves[0], halves[1])

    return pl.pallas_call(
        body,
        grid=(n_heads,),
        in_specs=[pl.BlockSpec((1, 2 * NA_WIN_ROWS - 1, GRID_W, 128), lambda h: (h, 0, 0, 0))],
        out_specs=pl.BlockSpec((3, 1, NA_KEY_ROWS * GRID_W, CHUNK), lambda h: (0, h, 0, 0)),
        out_shape=jax.ShapeDtypeStruct((3, n_heads, NA_KEY_ROWS * GRID_W, CHUNK), F32),
        compiler_params=pltpu.CompilerParams(
            dimension_semantics=("parallel",), vmem_limit_bytes=VMEM_LIMIT_BYTES),
        name="na_bias",
    )(table)


def _na_kernel(qt_ref, k_ref, vt_ref, km_ref, vm_ref, bias_ref, o_ref, s0_ref, s1_ref):
    qt = qt_ref[0, 0]
    k_all = k_ref[0]
    vt_all = vt_ref[0, 0]
    k_meta = km_ref[...]
    vt_meta = vm_ref[0]
    row = lax.broadcasted_iota(jnp.int32, qt.shape, 0)
    s_refs = (s0_ref, s1_ref)
    m = []
    s_meta = []
    for hh in range(2):
        in_head = (row >= hh * NA_HEAD_DIM) & (row < (hh + 1) * NA_HEAD_DIM)
        qt_h = jnp.where(in_head, qt, jnp.zeros_like(qt))
        s = _dot(k_all, qt_h) + bias_ref[0, hh]
        s_refs[hh][...] = s
        s_meta.append(_dot(k_meta, qt_h))
        m.append(jnp.maximum(jnp.max(_col_part(s, jnp.max), axis=0, keepdims=True),
                             jnp.max(s_meta[hh], axis=0, keepdims=True)))
    outs = []
    for hh in range(2):
        p = jnp.exp2((s_refs[hh][...] - m[hh]).astype(BF16))
        p_meta = jnp.exp2((s_meta[hh] - m[hh]).astype(BF16))
        o = _dot(vt_all, p) + _dot(vt_meta, p_meta)
        outs.append(o[0:128] / o[128:129])
    o_ref[0, 0] = jnp.where(row < NA_HEAD_DIM, outs[0], outs[1]).astype(BF16)


def _na_attention(qt, k, vt, k_meta, vt_meta, bias):
    b, n_groups, _, _ = qt.shape
    n_keys = NA_KEY_ROWS * GRID_W
    t = n_groups * CHUNK

    def key_start(g):
        quarter = n_keys // 4
        return jnp.clip(2 * g - 1, 0, (t - n_keys) // quarter) * quarter

    def variant(g):
        return jnp.where(g == 0, 0, jnp.where(g == n_groups - 1, 2, 1))

    return pl.pallas_call(
        _na_kernel,
        grid=(NA_HEADS // 2, n_groups, b),
        in_specs=[
            pl.BlockSpec((1, 1, 128, CHUNK), lambda hp, g, bi: (bi, g, hp, 0)),
            pl.BlockSpec((pl.Element(1), pl.Element(n_keys), pl.Element(128)),
                         lambda hp, g, bi: (bi, key_start(g), hp * 128)),
            pl.BlockSpec((pl.Element(1), pl.Element(1), pl.Element(V_ROWS), pl.Element(n_keys)),
                         lambda hp, g, bi: (bi, hp, 0, key_start(g))),
            pl.BlockSpec((N_META, 128), lambda hp, g, bi: (0, hp)),
            pl.BlockSpec((1, V_ROWS, N_META), lambda hp, g, bi: (hp, 0, 0)),
            pl.BlockSpec((1, 2, NA_KEY_ROWS * GRID_W, CHUNK),
                         lambda hp, g, bi: (variant(g), hp, 0, 0)),
        ],
        out_specs=pl.BlockSpec((1, 1, 128, CHUNK), lambda hp, g, bi: (bi, g, hp, 0)),
        out_shape=jax.ShapeDtypeStruct(qt.shape, BF16),
        scratch_shapes=[pltpu.VMEM((NA_KEY_ROWS * GRID_W, CHUNK), F32)] * 2,
        compiler_params=pltpu.CompilerParams(
            dimension_semantics=("parallel", "parallel", "parallel"),
            vmem_limit_bytes=VMEM_LIMIT_BYTES),
        name="na_attn",
    )(qt, k, vt, k_meta, vt_meta, bias)


def _col_part(x, op):
    return op(x.reshape(x.shape[0] // 8, 8, x.shape[1]), axis=0)


def _df_kernel(qt_ref, k_ref, vt_ref, km_ref, vm_ref, lam_ref, g_ref, o_ref,
               s_a0, s_a1, s_b0, s_b1, p_a0, p_a1, p_b0, p_b1, acc0_ref, acc1_ref, *, n_chunks):
    s_even, s_odd = (s_a0, s_a1), (s_b0, s_b1)
    p_even, p_odd = (p_a0, p_a1), (p_b0, p_b1)
    acc_refs = (acc0_ref, acc1_ref)
    n_q = n_chunks
    k_meta = km_ref[...]
    vt_meta = vm_ref[0]
    lam_v = lam_ref[...]
    lam = (jnp.exp(jnp.sum(lam_v[0:1] * lam_v[1:2], axis=1, keepdims=True))
           - jnp.exp(jnp.sum(lam_v[2:3] * lam_v[3:4], axis=1, keepdims=True)) + LAMBDA_INIT)

    def masked_q(qb):
        qt = qt_ref[0, qb]
        row = lax.broadcasted_iota(jnp.int32, qt.shape, 0)
        zero = jnp.zeros_like(qt)
        return (jnp.where(row < DIFF_HEAD_DIM, qt, zero), jnp.where(row >= DIFF_HEAD_DIM, qt, zero))

    def qk(qts, chunk, s_bufs):
        kc = k_ref[0, chunk]
        part = []
        for c in range(2):
            s = _dot(kc, qts[c])
            s_bufs[c][...] = s
            part.append(_col_part(s, jnp.max))
        return part

    def new_max(m, part):
        m_new = [jnp.maximum(m[c], jnp.max(part[c], axis=0, keepdims=True)) for c in range(2)]
        return m_new, [jnp.exp2(m[c] - m_new[c]) for c in range(2)]

    def expo(s_bufs, p_bufs, m):
        for c in range(2):
            p_bufs[c][...] = jnp.exp2((s_bufs[c][...] - m[c]).astype(BF16))

    def pv(p_bufs, chunk, alpha):
        vc = vt_ref[0, chunk, 0]
        for c in range(2):
            acc_refs[c][...] = acc_refs[c][...] * alpha[c] + _dot(vc, p_bufs[c][...])

    def finish(qb):
        o = (acc0_ref[0:128] / acc0_ref[128:129]
             - lam * (acc1_ref[0:128] / acc1_ref[128:129]))
        o = o * lax.rsqrt(jnp.mean(o * o, axis=0, keepdims=True) + SUBLN_EPS)
        o_ref[0, qb] = (o * g_ref[...] * (1.0 - LAMBDA_INIT)).astype(BF16)

    def handover(qb, cur):
        nxt = jnp.minimum(qb + 1, n_q - 1)
        qts = masked_q(nxt)
        part0 = qk(qts, 0, s_even)
        s_meta = [_dot(k_meta, qts[c]) for c in range(2)]
        if cur is not None:
            expo(s_odd, p_odd, cur[0])
            pv(p_even, n_chunks - 2, cur[2])
        m0 = [jnp.maximum(jnp.max(part0[c], axis=0, keepdims=True),
                          jnp.max(s_meta[c], axis=0, keepdims=True)) for c in range(2)]
        part1 = qk(qts, 1, s_odd)
        expo(s_even, p_even, m0)
        if cur is not None:
            pv(p_odd, n_chunks - 1, cur[1])
            finish(qb)
        for c in range(2):
            acc_refs[c][...] = _dot(vt_meta, jnp.exp2((s_meta[c] - m0[c]).astype(BF16)))
        m1, a1 = new_max(m0, part1)
        return tuple(m1), tuple(a1), tuple(jnp.ones_like(x) for x in m1)

    def steps(qts, t, m_e, a_e, a_p):
        part = qk(qts, t + 2, s_even)
        expo(s_odd, p_odd, m_e)
        pv(p_even, t, a_p)
        m_2, a_2 = new_max(m_e, part)
        part = qk(qts, t + 3, s_odd)
        expo(s_even, p_even, m_2)
        pv(p_odd, t + 1, a_e)
        m_3, a_3 = new_max(m_2, part)
        return tuple(m_3), tuple(a_3), tuple(a_2)

    def block(qb, carry):
        qts = masked_q(qb)
        for i in range(n_chunks // 2 - 1):
            carry = steps(qts, 2 * i, *carry)
        return handover(qb, carry)

    assert n_chunks % 2 == 0 and n_chunks >= 4
    lax.fori_loop(0, n_q, block, handover(-1, None))


def _df_attention(qt, k, vt, k_meta, vt_meta, lam_vecs, subln_col):
    b, n_chunks, _, _ = qt.shape
    k4 = k.reshape(b, n_chunks, CHUNK, DIFF_WIDTH)
    return pl.pallas_call(
        functools.partial(_df_kernel, n_chunks=n_chunks),
        grid=(b, DIFF_HEADS),
        in_specs=[
            pl.BlockSpec((1, n_chunks, 128, CHUNK), lambda bi, h: (bi, 0, h, 0)),
            pl.BlockSpec((1, n_chunks, CHUNK, 128), lambda bi, h: (bi, 0, 0, h)),
            pl.BlockSpec((1, n_chunks, 1, V_ROWS, CHUNK), lambda bi, h: (bi, 0, h, 0, 0)),
            pl.BlockSpec((N_META, 128), lambda bi, h: (0, h)),
            pl.BlockSpec((1, V_ROWS, N_META), lambda bi, h: (h, 0, 0)),
            pl.BlockSpec((4, DIFF_HEAD_DIM), lambda bi, h: (0, 0)),
            pl.BlockSpec((128, 1), lambda bi, h: (0, 0)),
        ],
        out_specs=pl.BlockSpec((1, n_chunks, 128, CHUNK), lambda bi, h: (bi, 0, h, 0)),
        out_shape=jax.ShapeDtypeStruct(qt.shape, BF16),
        scratch_shapes=[pltpu.VMEM((CHUNK, CHUNK), F32)] * 4
        + [pltpu.VMEM((CHUNK, CHUNK), BF16)] * 4
        + [pltpu.VMEM((V_ROWS, CHUNK), F32)] * 2,
        compiler_params=pltpu.CompilerParams(
            dimension_semantics=("parallel", "parallel"), vmem_limit_bytes=VMEM_LIMIT_BYTES),
        name="df_attn",
    )(qt, k4, vt, k_meta, vt_meta, lam_vecs, subln_col)


def _out_ffn_kernel(x_ref, nat_ref, dft_ref, g_na_ref, g_df_ref, w_na_ref, w_df_ref, w_o_ref,
                    ffn_g_ref, w_gate_ref, w_up_ref, w_down_ref, fin_g_ref, o_ref):
    o_na = _dot_tn(nat_ref[0, 0], w_na_ref[...])
    o_df = _dot_tn(dft_ref[0, 0], w_df_ref[...])
    merged = g_na_ref[0].astype(F32) * o_na + g_df_ref[0].astype(F32) * o_df
    x1 = x_ref[0] + _dot(merged.astype(BF16), w_o_ref[...])
    h = (_rms_scale(x1, NORM_EPS) * ffn_g_ref[...]).astype(BF16)
    gate = _dot(h, w_gate_ref[...])
    up = _dot(h, w_up_ref[...])
    act = (gate * jax.nn.sigmoid(gate) * up).astype(BF16)
    x2 = x1 + _dot(act, w_down_ref[...])
    o_ref[0] = _rms_scale(x2, NORM_EPS) * fin_g_ref[...]


def _out_ffn(x, nat, dft, g_na, g_df, w_na, w_df, w_o, ffn_g, w_gate, w_up, w_down, fin_g, rows):
    b, t, _ = x.shape
    per_chunk = CHUNK // rows
    const = lambda shape: pl.BlockSpec(shape, lambda i, j: (0,) * len(shape),
                                       pipeline_mode=pl.Buffered(1))
    tok_spec = pl.BlockSpec((1, rows, D_MODEL), lambda i, j: (i, j, 0))
    feat_spec = pl.BlockSpec((1, 1, 512, rows), lambda i, j: (i, j // per_chunk, 0, j % per_chunk))
    return pl.pallas_call(
        _out_ffn_kernel,
        grid=(b, t // rows),
        in_specs=[tok_spec, feat_spec, feat_spec, tok_spec, tok_spec,
                  const((NA_WIDTH, D_MODEL)), const((DIFF_WIDTH, D_MODEL)), const((D_MODEL, D_MODEL)),
                  const((1, D_MODEL)), const((D_MODEL, D_FF)), const((D_MODEL, D_FF)),
                  const((D_FF, D_MODEL)), const((1, D_MODEL))],
        out_specs=tok_spec,
        out_shape=jax.ShapeDtypeStruct(x.shape, F32),
        compiler_params=pltpu.CompilerParams(
            dimension_semantics=("parallel", "parallel"), vmem_limit_bytes=VMEM_LIMIT_BYTES),
        name="out_ffn",
    )(x, nat, dft, g_na, g_df, w_na, w_df, w_o, ffn_g, w_gate, w_up, w_down, fin_g)


def kernel(x, meta_tokens, mix_norm, w_in, na_rpb, lambda_q1, lambda_k1, lambda_q2, lambda_k2,
           diff_subln, w_na_out, w_diff_out, w_o, ffn_norm, w_gate, w_up, w_down, final_norm):
    b, t, _ = x.shape
    assert mix_norm.shape[0] == 1, "single-layer block"
    assert t % CHUNK == 0 and t // CHUNK >= 3

    w = w_in[0].astype(BF16)
    na_q, na_k, na_v, df_q, df_k, df_v, g_na, g_df = jnp.split(
        w, [512, 1024, 1536, 2048, 2560, 3072, 4096], axis=1)
    w_t = jnp.concatenate([na_q, na_v, df_q, df_v], axis=1).T
    w_s = jnp.concatenate([na_k, df_k, g_na, g_df], axis=1)
    gain = mix_norm[0][None].astype(F32)

    pos = jnp.arange(N_META + t, dtype=jnp.int32)
    tables_meta = _rope_tables(pos[:N_META])
    tables_real = _rope_tables(pos[N_META:])

    na_qt, na_vt, df_qt, df_vt, na_k_r, df_k_r, g_na_r, g_df_r = _in_proj(
        x, gain, w_t, w_s, *tables_real, rows=CHUNK)
    meta = _in_proj(meta_tokens[None].astype(x.dtype), gain, w_t, w_s, *tables_meta, rows=N_META)
    na_vt_m, df_vt_m, na_k_m, df_k_m = meta[1][0], meta[3][0, 0], meta[4][0], meta[5][0]

    bias = _na_bias_tiles(na_rpb[0])
    na_out_t = _na_attention(na_qt, na_k_r, na_vt, na_k_m, na_vt_m, bias)

    lam_vecs = jnp.stack([lambda_q1[0], lambda_k1[0], lambda_q2[0], lambda_k2[0]]).astype(F32)
    df_out_t = _df_attention(df_qt, df_k_r, df_vt, df_k_m, df_vt_m, lam_vecs,
                             diff_subln[0].astype(F32)[:, None])

    return _out_ffn(x, na_out_t, df_out_t, g_na_r, g_df_r,
                    w_na_out[0].astype(BF16), w_diff_out[0].astype(BF16), w_o[0].astype(BF16),
                    ffn_norm[0][None].astype(F32), w_gate[0].astype(BF16), w_up[0].astype(BF16),
                    w_down[0].astype(BF16), final_norm[None].astype(F32), rows=256)
```

```python
import functools
import math

import jax
import jax.numpy as jnp
from jax import lax
from jax.experimental import pallas as pl
from jax.experimental.pallas import tpu as pltpu

D_MODEL = 1024
GRID_W = 64
N_META = 16
NA_HEADS = 8
NA_HEAD_DIM = 64
NA_WIN_ROWS = 8
NA_WIN_COLS = 16
DIFF_HEADS = 4
DIFF_HEAD_DIM = 64
NA_WIDTH = NA_HEADS * NA_HEAD_DIM
DIFF_WIDTH = DIFF_HEADS * 2 * DIFF_HEAD_DIM
D_FF = 2816
ROPE_THETA = 10000.0
NORM_EPS = 1e-6
SUBLN_EPS = 1e-5
LAMBDA_INIT = 0.8 - 0.6 * math.exp(-0.3 * 0)

CHUNK = 512
NA_GROUP_ROWS = CHUNK // GRID_W
NA_KEY_ROWS = 16
NA_KEY_BLOCK = 256
V_ROWS = 128 + 16
LOG2_E = math.log2(math.e)
MASK_VALUE = -1e30
VMEM_LIMIT_BYTES = 56 * 1024 * 1024

BF16 = jnp.bfloat16
F32 = jnp.float32


def _dot(a, b):
    return jnp.dot(a, b, preferred_element_type=F32)


def _dot_nt(a, b):
    return lax.dot_general(a, b, (((1,), (1,)), ((), ())), preferred_element_type=F32)


def _dot_tn(a, b):
    return lax.dot_general(a, b, (((0,), (0,)), ((), ())), preferred_element_type=F32)


def _rms_scale(x, eps):
    return x * lax.rsqrt(jnp.mean(x * x, axis=-1, keepdims=True) + eps)


def _in_proj_kernel(x_ref, g_ref, wt_ref, ws_ref, cos_t_ref, sin_t_ref, cos_s_ref, sin_s_ref,
                    na_qt_ref, na_vt_ref, df_qt_ref, df_vt_ref, na_k_ref, df_k_ref,
                    g_na_ref, g_df_ref):
    h = (_rms_scale(x_ref[0], NORM_EPS) * g_ref[...]).astype(BF16)

    na_qt_ref[0, 0] = (_dot_nt(wt_ref[0:512], h) * (LOG2_E * NA_HEAD_DIM ** -0.5)).astype(BF16)
    nv = _dot_nt(wt_ref[512:1024], h).astype(BF16)
    dv = _dot_nt(wt_ref[1536:2048], h).astype(BF16)
    pad_row = lax.broadcasted_iota(jnp.int32, (V_ROWS - 128, dv.shape[1]), 0)
    ones_pad = jnp.where(pad_row == 0, 1.0, 0.0).astype(BF16)
    for hd in range(DIFF_HEADS):
        na_vt_ref[0, hd, 0:128, :] = nv[hd * 128:(hd + 1) * 128]
        na_vt_ref[0, hd, 128:V_ROWS, :] = ones_pad
        df_vt_ref[0, 0, hd, 0:128, :] = dv[hd * 128:(hd + 1) * 128]
        df_vt_ref[0, 0, hd, 128:V_ROWS, :] = ones_pad
    dq = _dot_nt(wt_ref[1024:1536], h) * (LOG2_E * DIFF_HEAD_DIM ** -0.5)
    cos_t = cos_t_ref[...]
    sin_t = sin_t_ref[...]
    half = DIFF_HEAD_DIM // 2
    for c in range(DIFF_WIDTH // DIFF_HEAD_DIM):
        x1 = dq[c * 64:c * 64 + half]
        x2 = dq[c * 64 + half:(c + 1) * 64]
        df_qt_ref[0, 0, c * 64:c * 64 + half, :] = (x1 * cos_t - x2 * sin_t).astype(BF16)
        df_qt_ref[0, 0, c * 64 + half:(c + 1) * 64, :] = (x2 * cos_t + x1 * sin_t).astype(BF16)

    na_k_ref[0] = _dot(h, ws_ref[:, 0:512]).astype(BF16)
    dk = _dot(h, ws_ref[:, 512:1024])
    cos_s = cos_s_ref[...]
    sin_s = sin_s_ref[...]
    lane = lax.broadcasted_iota(jnp.int32, cos_s.shape, 1)
    low_half = (lane % DIFF_HEAD_DIM) < half
    for c in range(DIFF_WIDTH // 128):
        xk = dk[:, c * 128:(c + 1) * 128]
        partner = jnp.where(low_half, pltpu.roll(xk, 128 - half, 1), pltpu.roll(xk, half, 1))
        df_k_ref[0, :, c * 128:(c + 1) * 128] = (xk * cos_s + partner * sin_s).astype(BF16)
    g_na_ref[0] = jax.nn.sigmoid(_dot(h, ws_ref[:, 1024:2048])).astype(BF16)
    g_df_ref[0] = jax.nn.sigmoid(_dot(h, ws_ref[:, 2048:3072])).astype(BF16)


def _in_proj(x, gain, w_t, w_s, cos_t, sin_t, cos_s, sin_s, rows):
    b, t, _ = x.shape
    n = t // rows
    const = lambda shape: pl.BlockSpec(shape, lambda i, j: (0,) * len(shape),
                                       pipeline_mode=pl.Buffered(1))
    feat_major = jax.ShapeDtypeStruct((b, n, 512, rows), BF16)
    feat_spec = pl.BlockSpec((1, 1, 512, rows), lambda i, j: (i, j, 0, 0))
    value_major = jax.ShapeDtypeStruct((b, n, 4, V_ROWS, rows), BF16)
    value_spec = pl.BlockSpec((1, 1, 4, V_ROWS, rows), lambda i, j: (i, j, 0, 0, 0))
    tok = lambda width: jax.ShapeDtypeStruct((b, t, width), BF16)
    tok_spec = lambda width: pl.BlockSpec((1, rows, width), lambda i, j: (i, j, 0))
    return pl.pallas_call(
        _in_proj_kernel,
        grid=(b, n),
        in_specs=[
            pl.BlockSpec((1, rows, D_MODEL), lambda i, j: (i, j, 0)),
            const((1, D_MODEL)),
            const((2048, D_MODEL)),
            const((D_MODEL, 3072)),
            pl.BlockSpec((DIFF_HEAD_DIM // 2, rows), lambda i, j: (0, j)),
            pl.BlockSpec((DIFF_HEAD_DIM // 2, rows), lambda i, j: (0, j)),
            pl.BlockSpec((rows, 128), lambda i, j: (j, 0)),
            pl.BlockSpec((rows, 128), lambda i, j: (j, 0)),
        ],
        out_specs=[feat_spec,
                   pl.BlockSpec((1, 4, V_ROWS, rows), lambda i, j: (i, 0, 0, j)),
                   feat_spec, value_spec,
                   tok_spec(512), tok_spec(512), tok_spec(D_MODEL), tok_spec(D_MODEL)],
        out_shape=[feat_major, jax.ShapeDtypeStruct((b, 4, V_ROWS, t), BF16), feat_major, value_major,
                   tok(512), tok(512), tok(D_MODEL), tok(D_MODEL)],
        compiler_params=pltpu.CompilerParams(
            dimension_semantics=("parallel", "parallel"), vmem_limit_bytes=VMEM_LIMIT_BYTES),
        name="in_proj",
    )(x, gain, w_t, w_s, cos_t, sin_t, cos_s, sin_s)


def _rope_tables(pos):
    half = DIFF_HEAD_DIM // 2
    inv = ROPE_THETA ** (-jnp.arange(half, dtype=F32) / half)
    ang = pos.astype(F32)[:, None] * inv[None, :]
    cos, sin = jnp.cos(ang), jnp.sin(ang)
    cos_s = jnp.tile(cos, (1, 128 // half))
    sin_s = jnp.tile(jnp.concatenate([-sin, sin], axis=1), (1, 128 // DIFF_HEAD_DIM))
    return cos.T, sin.T, cos_s, sin_s


def _na_bias_tiles(rpb):
    n_heads, n_dr, n_dc = rpb.shape
    mid = NA_WIN_COLS - 1
    ring = jnp.concatenate([rpb[:, :, mid::-1].astype(F32),
                            jnp.zeros((n_heads, n_dr, 128 - n_dc), F32),
                            rpb[:, :, :mid:-1].astype(F32)], axis=-1)

    def first(i, j):
        return i - j if max(j - 4, 0) <= i < max(j - 4, 0) + NA_WIN_ROWS else None

    def inner(i, j):
        return i - j - 4 if 0 <= i - j < NA_WIN_ROWS else None

    def last(i, j):
        return i - j - 8 if min(j + 4, 8) <= i < min(j + 4, 8) + NA_WIN_ROWS else None

    def body(ring_ref, o_ref):
        key_col = lax.broadcasted_iota(jnp.int32, (GRID_W, 128), 0)
        lane = lax.broadcasted_iota(jnp.int32, (GRID_W, 128), 1)
        col_start = jnp.clip(lane % GRID_W - NA_WIN_COLS // 2, 0, GRID_W - NA_WIN_COLS)
        col_ok = (key_col >= col_start) & (key_col < col_start + NA_WIN_COLS)
        masked = jnp.full((GRID_W, 128), MASK_VALUE, F32)
        tables = []
        for d in range(n_dr):
            x = jnp.broadcast_to(ring_ref[0, d:d + 1, :], (GRID_W, 128)) * LOG2_E
            for bit in range(GRID_W.bit_length() - 1):
                x = jnp.where((key_col >> bit) & 1 == 1, pltpu.roll(x, 1 << bit, 1), x)
            x = jnp.where(lane < GRID_W, x, pltpu.roll(x, GRID_W, 1))
            tables.append(jnp.where(col_ok, x, masked))
        for v, offset in enumerate((first, inner, last)):
            for i in range(NA_KEY_ROWS):
                for jp in range(NA_GROUP_ROWS // 2):
                    halves = []
                    for j in (2 * jp, 2 * jp + 1):
                        dr = offset(i, j)
                        halves.append(masked if dr is None else tables[dr + NA_WIN_ROWS - 1])
                    o_ref[v, 0, i * GRID_W:(i + 1) * GRID_W, jp * 128:(jp + 1) * 128] = jnp.where(
                        lane < GRID_W, halves[0], halves[1])

    return pl.pallas_call(
        body,
        grid=(n_heads,),
        in_specs=[pl.BlockSpec((1, n_dr, 128), lambda h: (h, 0, 0))],
        out_specs=pl.BlockSpec((3, 1, NA_KEY_ROWS * GRID_W, CHUNK), lambda h: (0, h, 0, 0)),
        out_shape=jax.ShapeDtypeStruct((3, n_heads, NA_KEY_ROWS * GRID_W, CHUNK), F32),
        compiler_params=pltpu.CompilerParams(
            dimension_semantics=("parallel",), vmem_limit_bytes=VMEM_LIMIT_BYTES),
        name="na_bias",
    )(ring)


def _na_kernel(qt_ref, k_ref, vt_ref, km_ref, vm_ref, bias_ref, o_ref, s0_ref, s1_ref):
    qt = qt_ref[0, 0]
    k_all = k_ref[0]
    vt_all = vt_ref[0, 0]
    k_meta = km_ref[...]
    vt_meta = vm_ref[0]
    row = lax.broadcasted_iota(jnp.int32, qt.shape, 0)
    s_refs = (s0_ref, s1_ref)
    m = []
    s_meta = []
    for hh in range(2):
        in_head = (row >= hh * NA_HEAD_DIM) & (row < (hh + 1) * NA_HEAD_DIM)
        qt_h = jnp.where(in_head, qt, jnp.zeros_like(qt))
        s = _dot(k_all, qt_h) + bias_ref[0, hh]
        s_refs[hh][...] = s
        s_meta.append(_dot(k_meta, qt_h))
        m.append(jnp.maximum(jnp.max(_col_part(s, jnp.max), axis=0, keepdims=True),
                             jnp.max(s_meta[hh], axis=0, keepdims=True)))
    outs = []
    for hh in range(2):
        p = jnp.exp2((s_refs[hh][...] - m[hh]).astype(BF16))
        p_meta = jnp.exp2((s_meta[hh] - m[hh]).astype(BF16))
        o = _dot(vt_all, p) + _dot(vt_meta, p_meta)
        outs.append(o[0:128] / o[128:129])
    o_ref[0, 0] = jnp.where(row < NA_HEAD_DIM, outs[0], outs[1]).astype(BF16)


def _na_attention(qt, k, vt, k_meta, vt_meta, bias):
    b, n_groups, _, _ = qt.shape
    n_keys = NA_KEY_ROWS * GRID_W
    t = n_groups * CHUNK

    def key_start(g):
        quarter = n_keys // 4
        return jnp.clip(2 * g - 1, 0, (t - n_keys) // quarter) * quarter

    def variant(g):
        return jnp.where(g == 0, 0, jnp.where(g == n_groups - 1, 2, 1))

    return pl.pallas_call(
        _na_kernel,
        grid=(NA_HEADS // 2, n_groups, b),
        in_specs=[
            pl.BlockSpec((1, 1, 128, CHUNK), lambda hp, g, bi: (bi, g, hp, 0)),
            pl.BlockSpec((pl.Element(1), pl.Element(n_keys), pl.Element(128)),
                         lambda hp, g, bi: (bi, key_start(g), hp * 128)),
            pl.BlockSpec((pl.Element(1), pl.Element(1), pl.Element(V_ROWS), pl.Element(n_keys)),
                         lambda hp, g, bi: (bi, hp, 0, key_start(g))),
            pl.BlockSpec((N_META, 128), lambda hp, g, bi: (0, hp)),
            pl.BlockSpec((1, V_ROWS, N_META), lambda hp, g, bi: (hp, 0, 0)),
            pl.BlockSpec((1, 2, NA_KEY_ROWS * GRID_W, CHUNK),
                         lambda hp, g, bi: (variant(g), hp, 0, 0)),
        ],
        out_specs=pl.BlockSpec((1, 1, 128, CHUNK), lambda hp, g, bi: (bi, g, hp, 0)),
        out_shape=jax.ShapeDtypeStruct(qt.shape, BF16),
        scratch_shapes=[pltpu.VMEM((NA_KEY_ROWS * GRID_W, CHUNK), F32)] * 2,
        compiler_params=pltpu.CompilerParams(
            dimension_semantics=("parallel", "parallel", "parallel"),
            vmem_limit_bytes=VMEM_LIMIT_BYTES),
        name="na_attn",
    )(qt, k, vt, k_meta, vt_meta, bias)


def _col_part(x, op):
    return op(x.reshape(x.shape[0] // 8, 8, x.shape[1]), axis=0)


def _df_kernel(qt_ref, k_ref, vt_ref, km_ref, vm_ref, lam_ref, g_ref, o_ref,
               s_a0, s_a1, s_b0, s_b1, p_a0, p_a1, p_b0, p_b1, acc0_ref, acc1_ref, *, n_chunks):
    s_even, s_odd = (s_a0, s_a1), (s_b0, s_b1)
    p_even, p_odd = (p_a0, p_a1), (p_b0, p_b1)
    acc_refs = (acc0_ref, acc1_ref)
    n_q = n_chunks
    k_meta = km_ref[...]
    vt_meta = vm_ref[0]
    lam_v = lam_ref[...]
    lam = (jnp.exp(jnp.sum(lam_v[0:1] * lam_v[1:2], axis=1, keepdims=True))
           - jnp.exp(jnp.sum(lam_v[2:3] * lam_v[3:4], axis=1, keepdims=True)) + LAMBDA_INIT)

    def masked_q(qb):
        qt = qt_ref[0, qb]
        row = lax.broadcasted_iota(jnp.int32, qt.shape, 0)
        zero = jnp.zeros_like(qt)
        return (jnp.where(row < DIFF_HEAD_DIM, qt, zero), jnp.where(row >= DIFF_HEAD_DIM, qt, zero))

    def qk(qts, chunk, s_bufs):
        kc = k_ref[0, chunk]
        part = []
        for c in range(2):
            s = _dot(kc, qts[c])
            s_bufs[c][...] = s
            part.append(_col_part(s, jnp.max))
        return part

    def new_max(m, part):
        m_new = [jnp.maximum(m[c], jnp.max(part[c], axis=0, keepdims=True)) for c in range(2)]
        return m_new, [jnp.exp2(m[c] - m_new[c]) for c in range(2)]

    def expo(s_bufs, p_bufs, m):
        for c in range(2):
            p_bufs[c][...] = jnp.exp2((s_bufs[c][...] - m[c]).astype(BF16))

    def pv(p_bufs, chunk, alpha):
        vc = vt_ref[0, chunk, 0]
        for c in range(2):
            acc_refs[c][...] = acc_refs[c][...] * alpha[c] + _dot(vc, p_bufs[c][...])

    def finish(qb):
        o = (acc0_ref[0:128] / acc0_ref[128:129]
             - lam * (acc1_ref[0:128] / acc1_ref[128:129]))
        o = o * lax.rsqrt(jnp.mean(o * o, axis=0, keepdims=True) + SUBLN_EPS)
        o_ref[0, qb] = (o * g_ref[...] * (1.0 - LAMBDA_INIT)).astype(BF16)

    def handover(qb, cur):
        nxt = jnp.minimum(qb + 1, n_q - 1)
        qts = masked_q(nxt)
        part0 = qk(qts, 0, s_even)
        s_meta = [_dot(k_meta, qts[c]) for c in range(2)]
        if cur is not None:
            expo(s_odd, p_odd, cur[0])
            pv(p_even, n_chunks - 2, cur[2])
        m0 = [jnp.maximum(jnp.max(part0[c], axis=0, keepdims=True),
                          jnp.max(s_meta[c], axis=0, keepdims=True)) for c in range(2)]
        part1 = qk(qts, 1, s_odd)
        expo(s_even, p_even, m0)
        if cur is not None:
            pv(p_odd, n_chunks - 1, cur[1])
            finish(qb)
        for c in range(2):
            acc_refs[c][...] = _dot(vt_meta, jnp.exp2((s_meta[c] - m0[c]).astype(BF16)))
        m1, a1 = new_max(m0, part1)
        return tuple(m1), tuple(a1), tuple(jnp.ones_like(x) for x in m1)

    def steps(qts, t, m_e, a_e, a_p):
        part = qk(qts, t + 2, s_even)
        expo(s_odd, p_odd, m_e)
        pv(p_even, t, a_p)
        m_2, a_2 = new_max(m_e, part)
        part = qk(qts, t + 3, s_odd)
        expo(s_even, p_even, m_2)
        pv(p_odd, t + 1, a_e)
        m_3, a_3 = new_max(m_2, part)
        return tuple(m_3), tuple(a_3), tuple(a_2)

    def block(qb, carry):
        qts = masked_q(qb)
        for i in range(n_chunks // 2 - 1):
            carry = steps(qts, 2 * i, *carry)
        return handover(qb, carry)

    assert n_chunks % 2 == 0 and n_chunks >= 4
    lax.fori_loop(0, n_q, block, handover(-1, None))


def _df_attention(qt, k, vt, k_meta, vt_meta, lam_vecs, subln_col):
    b, n_chunks, _, _ = qt.shape
    k4 = k.reshape(b, n_chunks, CHUNK, DIFF_WIDTH)
    return pl.pallas_call(
        functools.partial(_df_kernel, n_chunks=n_chunks),
        grid=(b, DIFF_HEADS),
        in_specs=[
            pl.BlockSpec((1, n_chunks, 128, CHUNK), lambda bi, h: (bi, 0, h, 0)),
            pl.BlockSpec((1, n_chunks, CHUNK, 128), lambda bi, h: (bi, 0, 0, h)),
            pl.BlockSpec((1, n_chunks, 1, V_ROWS, CHUNK), lambda bi, h: (bi, 0, h, 0, 0)),
            pl.BlockSpec((N_META, 128), lambda bi, h: (0, h)),
            pl.BlockSpec((1, V_ROWS, N_META), lambda bi, h: (h, 0, 0)),
            pl.BlockSpec((4, DIFF_HEAD_DIM), lambda bi, h: (0, 0)),
            pl.BlockSpec((128, 1), lambda bi, h: (0, 0)),
        ],
        out_specs=pl.BlockSpec((1, n_chunks, 128, CHUNK), lambda bi, h: (bi, 0, h, 0)),
        out_shape=jax.ShapeDtypeStruct(qt.shape, BF16),
        scratch_shapes=[pltpu.VMEM((CHUNK, CHUNK), F32)] * 4
        + [pltpu.VMEM((CHUNK, CHUNK), BF16)] * 4
        + [pltpu.VMEM((V_ROWS, CHUNK), F32)] * 2,
        compiler_params=pltpu.CompilerParams(
            dimension_semantics=("parallel", "parallel"), vmem_limit_bytes=VMEM_LIMIT_BYTES),
        name="df_attn",
    )(qt, k4, vt, k_meta, vt_meta, lam_vecs, subln_col)


def _out_ffn_kernel(x_ref, nat_ref, dft_ref, g_na_ref, g_df_ref, w_na_ref, w_df_ref, w_o_ref,
                    ffn_g_ref, w_gate_ref, w_up_ref, w_down_ref, fin_g_ref, o_ref):
    o_na = _dot_tn(nat_ref[0, 0], w_na_ref[...])
    o_df = _dot_tn(dft_ref[0, 0], w_df_ref[...])
    merged = g_na_ref[0].astype(F32) * o_na + g_df_ref[0].astype(F32) * o_df
    x1 = x_ref[0] + _dot(merged.astype(BF16), w_o_ref[...])
    h = (_rms_scale(x1, NORM_EPS) * ffn_g_ref[...]).astype(BF16)
    gate = _dot(h, w_gate_ref[...])
    up = _dot(h, w_up_ref[...])
    act = (gate * jax.nn.sigmoid(gate) * up).astype(BF16)
    x2 = x1 + _dot(act, w_down_ref[...])
    o_ref[0] = _rms_scale(x2, NORM_EPS) * fin_g_ref[...]


def _out_ffn(x, nat, dft, g_na, g_df, w_na, w_df, w_o, ffn_g, w_gate, w_up, w_down, fin_g, rows):
    b, t, _ = x.shape
    per_chunk = CHUNK // rows
    const = lambda shape: pl.BlockSpec(shape, lambda i, j: (0,) * len(shape),
                                       pipeline_mode=pl.Buffered(1))
    tok_spec = pl.BlockSpec((1, rows, D_MODEL), lambda i, j: (i, j, 0))
    feat_spec = pl.BlockSpec((1, 1, 512, rows), lambda i, j: (i, j // per_chunk, 0, j % per_chunk))
    return pl.pallas_call(
        _out_ffn_kernel,
        grid=(b, t // rows),
        in_specs=[tok_spec, feat_spec, feat_spec, tok_spec, tok_spec,
                  const((NA_WIDTH, D_MODEL)), const((DIFF_WIDTH, D_MODEL)), const((D_MODEL, D_MODEL)),
                  const((1, D_MODEL)), const((D_MODEL, D_FF)), const((D_MODEL, D_FF)),
                  const((D_FF, D_MODEL)), const((1, D_MODEL))],
        out_specs=tok_spec,
        out_shape=jax.ShapeDtypeStruct(x.shape, F32),
        compiler_params=pltpu.CompilerParams(
            dimension_semantics=("parallel", "parallel"), vmem_limit_bytes=VMEM_LIMIT_BYTES),
        name="out_ffn",
    )(x, nat, dft, g_na, g_df, w_na, w_df, w_o, ffn_g, w_gate, w_up, w_down, fin_g)


def kernel(x, meta_tokens, mix_norm, w_in, na_rpb, lambda_q1, lambda_k1, lambda_q2, lambda_k2,
           diff_subln, w_na_out, w_diff_out, w_o, ffn_norm, w_gate, w_up, w_down, final_norm):
    b, t, _ = x.shape
    assert mix_norm.shape[0] == 1, "single-layer block"
    assert t % CHUNK == 0 and t // CHUNK >= 3

    w = w_in[0].astype(BF16)
    na_q, na_k, na_v, df_q, df_k, df_v, g_na, g_df = jnp.split(
        w, [512, 1024, 1536, 2048, 2560, 3072, 4096], axis=1)
    w_t = jnp.concatenate([na_q, na_v, df_q, df_v], axis=1).T
    w_s = jnp.concatenate([na_k, df_k, g_na, g_df], axis=1)
    gain = mix_norm[0][None].astype(F32)

    pos = jnp.arange(N_META + t, dtype=jnp.int32)
    tables_meta = _rope_tables(pos[:N_META])
    tables_real = _rope_tables(pos[N_META:])

    na_qt, na_vt, df_qt, df_vt, na_k_r, df_k_r, g_na_r, g_df_r = _in_proj(
        x, gain, w_t, w_s, *tables_real, rows=CHUNK)
    meta = _in_proj(meta_tokens[None].astype(x.dtype), gain, w_t, w_s, *tables_meta, rows=N_META)
    na_vt_m, df_vt_m, na_k_m, df_k_m = meta[1][0], meta[3][0, 0], meta[4][0], meta[5][0]

    bias = _na_bias_tiles(na_rpb[0])
    na_out_t = _na_attention(na_qt, na_k_r, na_vt, na_k_m, na_vt_m, bias)

    lam_vecs = jnp.stack([lambda_q1[0], lambda_k1[0], lambda_q2[0], lambda_k2[0]]).astype(F32)
    df_out_t = _df_attention(df_qt, df_k_r, df_vt, df_k_m, df_vt_m, lam_vecs,
                             diff_subln[0].astype(F32)[:, None])

    return _out_ffn(x, na_out_t, df_out_t, g_na_r, g_df_r,
                    w_na_out[0].astype(BF16), w_diff_out[0].astype(BF16), w_o[0].astype(BF16),
                    ffn_norm[0][None].astype(F32), w_gate[0].astype(BF16), w_up[0].astype(BF16),
                    w_down[0].astype(BF16), final_norm[None].astype(F32), rows=512)
```

```python
import functools
import math

import jax
import jax.numpy as jnp
from jax import lax
from jax.experimental import pallas as pl
from jax.experimental.pallas import tpu as pltpu

D_MODEL = 1024
GRID_W = 64
N_META = 16
NA_HEADS = 8
NA_HEAD_DIM = 64
NA_WIN_ROWS = 8
NA_WIN_COLS = 16
DIFF_HEADS = 4
DIFF_HEAD_DIM = 64
NA_WIDTH = NA_HEADS * NA_HEAD_DIM
DIFF_WIDTH = DIFF_HEADS * 2 * DIFF_HEAD_DIM
D_FF = 2816
ROPE_THETA = 10000.0
NORM_EPS = 1e-6
SUBLN_EPS = 1e-5
LAMBDA_INIT = 0.8 - 0.6 * math.exp(-0.3 * 0)

CHUNK = 512
NA_GROUP_ROWS = CHUNK // GRID_W
NA_KEY_ROWS = 16
NA_PAIR = 2 * GRID_W
NA_PAIR_KEY_ROWS = NA_WIN_ROWS + 2
NA_QUAD = 4
NA_QUAD_DIM = NA_QUAD * NA_HEAD_DIM
NA_V_ROWS = NA_QUAD_DIM + 16
V_ROWS = 128 + 16
LOG2_E = math.log2(math.e)
MASK_VALUE = -1e30
VMEM_LIMIT_BYTES = 56 * 1024 * 1024

BF16 = jnp.bfloat16
F32 = jnp.float32


def _dot(a, b):
    return jnp.dot(a, b, preferred_element_type=F32)


def _dot_nt(a, b):
    return lax.dot_general(a, b, (((1,), (1,)), ((), ())), preferred_element_type=F32)


def _dot_tn(a, b):
    return lax.dot_general(a, b, (((0,), (0,)), ((), ())), preferred_element_type=F32)


def _rms_scale(x, eps):
    return x * lax.rsqrt(jnp.mean(x * x, axis=-1, keepdims=True) + eps)


def _in_proj_kernel(x_ref, g_ref, wt_ref, ws_ref, cos_t_ref, sin_t_ref, cos_s_ref, sin_s_ref,
                    na_qt_ref, na_vt_ref, df_qt_ref, df_vt_ref, na_k_ref, df_k_ref,
                    g_na_ref, g_df_ref):
    h = (_rms_scale(x_ref[0], NORM_EPS) * g_ref[...]).astype(BF16)

    na_qt_ref[0, 0] = (_dot_nt(wt_ref[0:512], h) * (LOG2_E * NA_HEAD_DIM ** -0.5)).astype(BF16)
    nv = _dot_nt(wt_ref[512:1024], h).astype(BF16)
    dv = _dot_nt(wt_ref[1536:2048], h).astype(BF16)
    pad_row = lax.broadcasted_iota(jnp.int32, (V_ROWS - 128, dv.shape[1]), 0)
    ones_pad = jnp.where(pad_row == 0, 1.0, 0.0).astype(BF16)
    for hd in range(DIFF_HEADS):
        df_vt_ref[0, 0, hd, 0:128, :] = dv[hd * 128:(hd + 1) * 128]
        df_vt_ref[0, 0, hd, 128:V_ROWS, :] = ones_pad
    pair = na_vt_ref.shape[-1]
    pad_row = lax.broadcasted_iota(jnp.int32, (NA_V_ROWS - NA_QUAD_DIM, pair), 0)
    ones_pair = jnp.where(pad_row == 0, 1.0, 0.0).astype(BF16)
    for quad in range(NA_HEADS // NA_QUAD):
        for pr in range(nv.shape[1] // pair):
            cols = slice(pr * pair, (pr + 1) * pair)
            na_vt_ref[0, quad, pr, 0:NA_QUAD_DIM, :] = nv[quad * NA_QUAD_DIM:(quad + 1) * NA_QUAD_DIM,
                                                          cols]
            na_vt_ref[0, quad, pr, NA_QUAD_DIM:NA_V_ROWS, :] = ones_pair
    dq = _dot_nt(wt_ref[1024:1536], h) * (LOG2_E * DIFF_HEAD_DIM ** -0.5)
    cos_t = cos_t_ref[...]
    sin_t = sin_t_ref[...]
    half = DIFF_HEAD_DIM // 2
    for c in range(DIFF_WIDTH // DIFF_HEAD_DIM):
        x1 = dq[c * 64:c * 64 + half]
        x2 = dq[c * 64 + half:(c + 1) * 64]
        df_qt_ref[0, 0, c * 64:c * 64 + half, :] = (x1 * cos_t - x2 * sin_t).astype(BF16)
        df_qt_ref[0, 0, c * 64 + half:(c + 1) * 64, :] = (x2 * cos_t + x1 * sin_t).astype(BF16)

    na_k_ref[0] = _dot(h, ws_ref[:, 0:512]).astype(BF16)
    dk = _dot(h, ws_ref[:, 512:1024])
    cos_s = cos_s_ref[...]
    sin_s = sin_s_ref[...]
    lane = lax.broadcasted_iota(jnp.int32, cos_s.shape, 1)
    low_half = (lane % DIFF_HEAD_DIM) < half
    for c in range(DIFF_WIDTH // 128):
        xk = dk[:, c * 128:(c + 1) * 128]
        partner = jnp.where(low_half, pltpu.roll(xk, 128 - half, 1), pltpu.roll(xk, half, 1))
        df_k_ref[0, :, c * 128:(c + 1) * 128] = (xk * cos_s + partner * sin_s).astype(BF16)
    g_na_ref[0] = jax.nn.sigmoid(_dot(h, ws_ref[:, 1024:2048])).astype(BF16)
    g_df_ref[0] = jax.nn.sigmoid(_dot(h, ws_ref[:, 2048:3072])).astype(BF16)


def _in_proj(x, gain, w_t, w_s, cos_t, sin_t, cos_s, sin_s, rows):
    b, t, _ = x.shape
    n = t // rows
    const = lambda shape: pl.BlockSpec(shape, lambda i, j: (0,) * len(shape),
                                       pipeline_mode=pl.Buffered(1))
    feat_major = jax.ShapeDtypeStruct((b, n, 512, rows), BF16)
    feat_spec = pl.BlockSpec((1, 1, 512, rows), lambda i, j: (i, j, 0, 0))
    pair = min(NA_PAIR, rows)
    n_quads = NA_HEADS // NA_QUAD
    value_major = jax.ShapeDtypeStruct((b, n, 4, V_ROWS, rows), BF16)
    value_spec = pl.BlockSpec((1, 1, 4, V_ROWS, rows), lambda i, j: (i, j, 0, 0, 0))
    tok = lambda width: jax.ShapeDtypeStruct((b, t, width), BF16)
    tok_spec = lambda width: pl.BlockSpec((1, rows, width), lambda i, j: (i, j, 0))
    return pl.pallas_call(
        _in_proj_kernel,
        grid=(b, n),
        in_specs=[
            pl.BlockSpec((1, rows, D_MODEL), lambda i, j: (i, j, 0)),
            const((1, D_MODEL)),
            const((2048, D_MODEL)),
            const((D_MODEL, 3072)),
            pl.BlockSpec((DIFF_HEAD_DIM // 2, rows), lambda i, j: (0, j)),
            pl.BlockSpec((DIFF_HEAD_DIM // 2, rows), lambda i, j: (0, j)),
            pl.BlockSpec((rows, 128), lambda i, j: (j, 0)),
            pl.BlockSpec((rows, 128), lambda i, j: (j, 0)),
        ],
        out_specs=[feat_spec,
                   pl.BlockSpec((1, n_quads, rows // pair, NA_V_ROWS, pair),
                                lambda i, j: (i, 0, j, 0, 0)),
                   feat_spec, value_spec,
                   tok_spec(512), tok_spec(512), tok_spec(D_MODEL), tok_spec(D_MODEL)],
        out_shape=[feat_major,
                   jax.ShapeDtypeStruct((b, n_quads, t // pair, NA_V_ROWS, pair), BF16),
                   feat_major, value_major,
                   tok(512), tok(512), tok(D_MODEL), tok(D_MODEL)],
        compiler_params=pltpu.CompilerParams(
            dimension_semantics=("parallel", "parallel"), vmem_limit_bytes=VMEM_LIMIT_BYTES),
        name="in_proj",
    )(x, gain, w_t, w_s, cos_t, sin_t, cos_s, sin_s)


def _rope_tables(pos):
    half = DIFF_HEAD_DIM // 2
    inv = ROPE_THETA ** (-jnp.arange(half, dtype=F32) / half)
    ang = pos.astype(F32)[:, None] * inv[None, :]
    cos, sin = jnp.cos(ang), jnp.sin(ang)
    cos_s = jnp.tile(cos, (1, 128 // half))
    sin_s = jnp.tile(jnp.concatenate([-sin, sin], axis=1), (1, 128 // DIFF_HEAD_DIM))
    return cos.T, sin.T, cos_s, sin_s


def _na_pair_base(first_row, n_rows):
    return jnp.clip(first_row - NA_WIN_ROWS // 2, 0, n_rows - NA_PAIR_KEY_ROWS)


def _na_pair_variants(n_rows):
    assert n_rows >= 2 * NA_PAIR_KEY_ROWS + 2
    kinds = []
    for first_row in (2 * NA_WIN_ROWS, 0, 2, n_rows - 4, n_rows - 2):
        base = min(max(first_row - NA_WIN_ROWS // 2, 0), n_rows - NA_PAIR_KEY_ROWS)
        rows = []
        for r in (first_row, first_row + 1):
            start = min(max(r - NA_WIN_ROWS // 2, 0), n_rows - NA_WIN_ROWS)
            rows.append((start - base, start - base + NA_WIN_ROWS, base - r))
        kinds.append(rows)
    return kinds


def _na_pair_kind(pair_index, n_pairs):
    return jnp.where(pair_index == 0, 1, jnp.where(pair_index == 1, 2, jnp.where(
        pair_index == n_pairs - 2, 3, jnp.where(pair_index == n_pairs - 1, 4, 0))))


def _na_bias_tiles(rpb, n_rows):
    n_heads, n_dr, n_dc = rpb.shape
    kinds = _na_pair_variants(n_rows)
    mid = NA_WIN_COLS - 1
    ring = jnp.concatenate([rpb[:, :, mid::-1].astype(F32),
                            jnp.zeros((n_heads, n_dr, 128 - n_dc), F32),
                            rpb[:, :, :mid:-1].astype(F32)], axis=-1)

    def body(ring_ref, o_ref):
        key_col = lax.broadcasted_iota(jnp.int32, (GRID_W, 128), 0)
        lane = lax.broadcasted_iota(jnp.int32, (GRID_W, 128), 1)
        col_start = jnp.clip(lane % GRID_W - NA_WIN_COLS // 2, 0, GRID_W - NA_WIN_COLS)
        col_ok = (key_col >= col_start) & (key_col < col_start + NA_WIN_COLS)
        masked = jnp.full((GRID_W, 128), MASK_VALUE, F32)
        tables = []
        for d in range(n_dr):
            x = jnp.broadcast_to(ring_ref[0, d:d + 1, :], (GRID_W, 128)) * LOG2_E
            for bit in range(GRID_W.bit_length() - 1):
                x = jnp.where((key_col >> bit) & 1 == 1, pltpu.roll(x, 1 << bit, 1), x)
            x = jnp.where(lane < GRID_W, x, pltpu.roll(x, GRID_W, 1))
            tables.append(jnp.where(col_ok, x, masked))
        for v, kind in enumerate(kinds):
            for i in range(NA_PAIR_KEY_ROWS):
                halves = [tables[shift + i + NA_WIN_ROWS - 1] if lo <= i < hi else masked
                          for lo, hi, shift in kind]
                o_ref[v, 0, i * GRID_W:(i + 1) * GRID_W, :] = jnp.where(
                    lane < GRID_W, halves[0], halves[1])

    n_keys = NA_PAIR_KEY_ROWS * GRID_W
    return pl.pallas_call(
        body,
        grid=(n_heads,),
        in_specs=[pl.BlockSpec((1, n_dr, 128), lambda h: (h, 0, 0))],
        out_specs=pl.BlockSpec((len(kinds), 1, n_keys, NA_PAIR),
                               lambda h: (0, h // NA_QUAD, 0, h % NA_QUAD)),
        out_shape=jax.ShapeDtypeStruct((len(kinds), n_heads // NA_QUAD, n_keys, NA_QUAD * NA_PAIR),
                                       F32),
        compiler_params=pltpu.CompilerParams(
            dimension_semantics=("parallel",), vmem_limit_bytes=VMEM_LIMIT_BYTES),
        name="na_bias",
    )(ring)


def _na_window_row(g, n_rows):
    quarter = NA_KEY_ROWS // 4
    return jnp.clip(2 * g - 1, 0, (n_rows - NA_KEY_ROWS) // quarter) * quarter


def _na_kernel(qt_ref, k_ref, vt_ref, km_ref, vm_ref, bias_ref, o_ref, *s_refs, n_rows):
    g = pl.program_id(1)
    n_pairs = n_rows // 2
    window_row = _na_window_row(g, n_rows)
    n_keys = NA_PAIR_KEY_ROWS * GRID_W
    head_of_row = lax.broadcasted_iota(jnp.int32, (NA_QUAD_DIM, NA_PAIR), 0) // NA_HEAD_DIM
    n_quads = NA_HEADS // NA_QUAD
    units = [(jp, quad) for jp in range(NA_GROUP_ROWS // 2) for quad in range(n_quads)]

    def local_row(jp):
        return _na_pair_base(2 * (g * (NA_GROUP_ROWS // 2) + jp), n_rows) - window_row

    def scores(u):
        jp, quad = units[u]
        kind = _na_pair_kind(g * (NA_GROUP_ROWS // 2) + jp, n_pairs)
        key_off = pl.multiple_of(local_row(jp) * GRID_W, NA_PAIR)
        feat = slice(quad * NA_QUAD_DIM, (quad + 1) * NA_QUAD_DIM)
        q_pair = qt_ref[0, 0, feat, jp * NA_PAIR:(jp + 1) * NA_PAIR]
        zero = jnp.zeros_like(q_pair)
        w = jnp.concatenate([jnp.where(head_of_row == hd, q_pair, zero) for hd in range(NA_QUAD)],
                            axis=1)
        s = _dot(k_ref[0, pl.ds(key_off, n_keys), feat], w) + bias_ref[kind, quad]
        s_refs[u][...] = s
        s_meta = _dot(km_ref[:, feat], w)
        m = jnp.maximum(jnp.max(_col_part(s, jnp.max), axis=0, keepdims=True),
                        jnp.max(s_meta, axis=0, keepdims=True))
        return s_meta, m

    def values(u, s_meta, m):
        jp, quad = units[u]
        slab_off = local_row(jp) // 2
        p = jnp.exp2((s_refs[u][...] - m).astype(BF16))
        p_meta = jnp.exp2((s_meta - m).astype(BF16))
        vt = jnp.concatenate([vt_ref[0, quad, slab_off + i] for i in range(NA_PAIR_KEY_ROWS // 2)],
                             axis=1)
        o = _dot(vt, p) + _dot(vm_ref[quad, 0], p_meta)
        o = o[0:NA_QUAD_DIM] / o[NA_QUAD_DIM:NA_QUAD_DIM + 1]
        for hd in range(NA_QUAD):
            o_ref[0, 0, quad * NA_QUAD_DIM + hd * NA_HEAD_DIM:
                  quad * NA_QUAD_DIM + (hd + 1) * NA_HEAD_DIM,
                  jp * NA_PAIR:(jp + 1) * NA_PAIR] = o[
                      hd * NA_HEAD_DIM:(hd + 1) * NA_HEAD_DIM,
                      hd * NA_PAIR:(hd + 1) * NA_PAIR].astype(BF16)

    pending = scores(0)
    for u in range(len(units)):
        following = scores(u + 1) if u + 1 < len(units) else None
        values(u, *pending)
        pending = following


def _na_attention(qt, k, vt, k_meta, vt_meta, bias):
    b, n_groups, _, _ = qt.shape
    n_rows = n_groups * NA_GROUP_ROWS
    n_quads = NA_HEADS // NA_QUAD
    n_kinds, _, n_keys, _ = bias.shape
    return pl.pallas_call(
        functools.partial(_na_kernel, n_rows=n_rows),
        grid=(b, n_groups),
        in_specs=[
            pl.BlockSpec((1, 1, NA_WIDTH, CHUNK), lambda bi, g: (bi, g, 0, 0)),
            pl.BlockSpec((pl.Element(1), pl.Element(NA_KEY_ROWS * GRID_W), pl.Element(NA_WIDTH)),
                         lambda bi, g: (bi, _na_window_row(g, n_rows) * GRID_W, 0)),
            pl.BlockSpec((pl.Element(1), pl.Element(n_quads), pl.Element(NA_KEY_ROWS // 2),
                          pl.Element(NA_V_ROWS), pl.Element(NA_PAIR)),
                         lambda bi, g: (bi, 0, _na_window_row(g, n_rows) // 2, 0, 0)),
            pl.BlockSpec((N_META, NA_WIDTH), lambda bi, g: (0, 0)),
            pl.BlockSpec((n_quads, 1, NA_V_ROWS, N_META), lambda bi, g: (0, 0, 0, 0)),
            pl.BlockSpec((n_kinds, n_quads, n_keys, NA_QUAD * NA_PAIR), lambda bi, g: (0, 0, 0, 0),
                         pipeline_mode=pl.Buffered(1)),
        ],
        out_specs=pl.BlockSpec((1, 1, NA_WIDTH, CHUNK), lambda bi, g: (bi, g, 0, 0)),
        out_shape=jax.ShapeDtypeStruct(qt.shape, BF16),
        scratch_shapes=[pltpu.VMEM((n_keys, NA_QUAD * NA_PAIR), F32)]
        * (n_quads * NA_GROUP_ROWS // 2),
        compiler_params=pltpu.CompilerParams(
            dimension_semantics=("parallel", "parallel"), vmem_limit_bytes=VMEM_LIMIT_BYTES),
        name="na_attn",
    )(qt, k, vt, k_meta, vt_meta, bias)


def _col_part(x, op):
    return op(x.reshape(x.shape[0] // 8, 8, x.shape[1]), axis=0)


def _df_kernel(qt_ref, k_ref, vt_ref, km_ref, vm_ref, lam_ref, g_ref, o_ref,
               s_a0, s_a1, s_b0, s_b1, p_a0, p_a1, p_b0, p_b1, acc0_ref, acc1_ref, *, n_chunks):
    s_even, s_odd = (s_a0, s_a1), (s_b0, s_b1)
    p_even, p_odd = (p_a0, p_a1), (p_b0, p_b1)
    acc_refs = (acc0_ref, acc1_ref)
    n_q = n_chunks
    k_meta = km_ref[...]
    vt_meta = vm_ref[0]
    lam_v = lam_ref[...]
    lam = (jnp.exp(jnp.sum(lam_v[0:1] * lam_v[1:2], axis=1, keepdims=True))
           - jnp.exp(jnp.sum(lam_v[2:3] * lam_v[3:4], axis=1, keepdims=True)) + LAMBDA_INIT)

    def masked_q(qb):
        qt = qt_ref[0, qb]
        row = lax.broadcasted_iota(jnp.int32, qt.shape, 0)
        zero = jnp.zeros_like(qt)
        return (jnp.where(row < DIFF_HEAD_DIM, qt, zero), jnp.where(row >= DIFF_HEAD_DIM, qt, zero))

    def qk(qts, chunk, s_bufs):
        kc = k_ref[0, chunk]
        part = []
        for c in range(2):
            s = _dot(kc, qts[c])
            s_bufs[c][...] = s
            part.append(_col_part(s, jnp.max))
        return part

    def new_max(m, part):
        m_new = [jnp.maximum(m[c], jnp.max(part[c], axis=0, keepdims=True)) for c in range(2)]
        return m_new, [jnp.exp2(m[c] - m_new[c]) for c in range(2)]

    def expo(s_bufs, p_bufs, m):
        for c in range(2):
            p_bufs[c][...] = jnp.exp2((s_bufs[c][...] - m[c]).astype(BF16))

    def pv(p_bufs, chunk, alpha):
        vc = vt_ref[0, chunk, 0]
        for c in range(2):
            acc_refs[c][...] = acc_refs[c][...] * alpha[c] + _dot(vc, p_bufs[c][...])

    def finish(qb):
        o = (acc0_ref[0:128] / acc0_ref[128:129]
             - lam * (acc1_ref[0:128] / acc1_ref[128:129]))
        o = o * lax.rsqrt(jnp.mean(o * o, axis=0, keepdims=True) + SUBLN_EPS)
        o_ref[0, qb] = (o * g_ref[...] * (1.0 - LAMBDA_INIT)).astype(BF16)

    def handover(qb, cur):
        nxt = jnp.minimum(qb + 1, n_q - 1)
        qts = masked_q(nxt)
        part0 = qk(qts, 0, s_even)
        s_meta = [_dot(k_meta, qts[c]) for c in range(2)]
        if cur is not None:
            expo(s_odd, p_odd, cur[0])
            pv(p_even, n_chunks - 2, cur[2])
        m0 = [jnp.maximum(jnp.max(part0[c], axis=0, keepdims=True),
                          jnp.max(s_meta[c], axis=0, keepdims=True)) for c in range(2)]
        part1 = qk(qts, 1, s_odd)
        expo(s_even, p_even, m0)
        if cur is not None:
            pv(p_odd, n_chunks - 1, cur[1])
            finish(qb)
        for c in range(2):
            acc_refs[c][...] = _dot(vt_meta, jnp.exp2((s_meta[c] - m0[c]).astype(BF16)))
        m1, a1 = new_max(m0, part1)
        return tuple(m1), tuple(a1), tuple(jnp.ones_like(x) for x in m1)

    def steps(qts, t, m_e, a_e, a_p):
        part = qk(qts, t + 2, s_even)
        expo(s_odd, p_odd, m_e)
        pv(p_even, t, a_p)
        m_2, a_2 = new_max(m_e, part)
        part = qk(qts, t + 3, s_odd)
        expo(s_even, p_even, m_2)
        pv(p_odd, t + 1, a_e)
        m_3, a_3 = new_max(m_2, part)
        return tuple(m_3), tuple(a_3), tuple(a_2)

    def block(qb, carry):
        qts = masked_q(qb)
        for i in range(n_chunks // 2 - 1):
            carry = steps(qts, 2 * i, *carry)
        return handover(qb, carry)

    assert n_chunks % 2 == 0 and n_chunks >= 4
    lax.fori_loop(0, n_q, block, handover(-1, None))


def _df_attention(qt, k, vt, k_meta, vt_meta, lam_vecs, subln_col):
    b, n_chunks, _, _ = qt.shape
    k4 = k.reshape(b, n_chunks, CHUNK, DIFF_WIDTH)
    return pl.pallas_call(
        functools.partial(_df_kernel, n_chunks=n_chunks),
        grid=(b, DIFF_HEADS),
        in_specs=[
            pl.BlockSpec((1, n_chunks, 128, CHUNK), lambda bi, h: (bi, 0, h, 0)),
            pl.BlockSpec((1, n_chunks, CHUNK, 128), lambda bi, h: (bi, 0, 0, h)),
            pl.BlockSpec((1, n_chunks, 1, V_ROWS, CHUNK), lambda bi, h: (bi, 0, h, 0, 0)),
            pl.BlockSpec((N_META, 128), lambda bi, h: (0, h)),
            pl.BlockSpec((1, V_ROWS, N_META), lambda bi, h: (h, 0, 0)),
            pl.BlockSpec((4, DIFF_HEAD_DIM), lambda bi, h: (0, 0)),
            pl.BlockSpec((128, 1), lambda bi, h: (0, 0)),
        ],
        out_specs=pl.BlockSpec((1, n_chunks, 128, CHUNK), lambda bi, h: (bi, 0, h, 0)),
        out_shape=jax.ShapeDtypeStruct(qt.shape, BF16),
        scratch_shapes=[pltpu.VMEM((CHUNK, CHUNK), F32)] * 4
        + [pltpu.VMEM((CHUNK, CHUNK), BF16)] * 4
        + [pltpu.VMEM((V_ROWS, CHUNK), F32)] * 2,
        compiler_params=pltpu.CompilerParams(
            dimension_semantics=("parallel", "parallel"), vmem_limit_bytes=VMEM_LIMIT_BYTES),
        name="df_attn",
    )(qt, k4, vt, k_meta, vt_meta, lam_vecs, subln_col)


def _out_ffn_kernel(x_ref, nat_ref, dft_ref, g_na_ref, g_df_ref, w_na_ref, w_df_ref, w_o_ref,
                    ffn_g_ref, w_gate_ref, w_up_ref, w_down_ref, fin_g_ref, o_ref):
    o_na = _dot_tn(nat_ref[0, 0], w_na_ref[...])
    o_df = _dot_tn(dft_ref[0, 0], w_df_ref[...])
    merged = g_na_ref[0].astype(F32) * o_na + g_df_ref[0].astype(F32) * o_df
    x1 = x_ref[0] + _dot(merged.astype(BF16), w_o_ref[...])
    h = (_rms_scale(x1, NORM_EPS) * ffn_g_ref[...]).astype(BF16)
    gate = _dot(h, w_gate_ref[...])
    up = _dot(h, w_up_ref[...])
    act = (gate * jax.nn.sigmoid(gate) * up).astype(BF16)
    x2 = x1 + _dot(act, w_down_ref[...])
    o_ref[0] = _rms_scale(x2, NORM_EPS) * fin_g_ref[...]


def _out_ffn(x, nat, dft, g_na, g_df, w_na, w_df, w_o, ffn_g, w_gate, w_up, w_down, fin_g, rows):
    b, t, _ = x.shape
    per_chunk = CHUNK // rows
    const = lambda shape: pl.BlockSpec(shape, lambda i, j: (0,) * len(shape),
                                       pipeline_mode=pl.Buffered(1))
    tok_spec = pl.BlockSpec((1, rows, D_MODEL), lambda i, j: (i, j, 0))
    feat_spec = pl.BlockSpec((1, 1, 512, rows), lambda i, j: (i, j // per_chunk, 0, j % per_chunk))
    return pl.pallas_call(
        _out_ffn_kernel,
        grid=(b, t // rows),
        in_specs=[tok_spec, feat_spec, feat_spec, tok_spec, tok_spec,
                  const((NA_WIDTH, D_MODEL)), const((DIFF_WIDTH, D_MODEL)), const((D_MODEL, D_MODEL)),
                  const((1, D_MODEL)), const((D_MODEL, D_FF)), const((D_MODEL, D_FF)),
                  const((D_FF, D_MODEL)), const((1, D_MODEL))],
        out_specs=tok_spec,
        out_shape=jax.ShapeDtypeStruct(x.shape, F32),
        compiler_params=pltpu.CompilerParams(
            dimension_semantics=("parallel", "parallel"), vmem_limit_bytes=VMEM_LIMIT_BYTES),
        name="out_ffn",
    )(x, nat, dft, g_na, g_df, w_na, w_df, w_o, ffn_g, w_gate, w_up, w_down, fin_g)


def kernel(x, meta_tokens, mix_norm, w_in, na_rpb, lambda_q1, lambda_k1, lambda_q2, lambda_k2,
           diff_subln, w_na_out, w_diff_out, w_o, ffn_norm, w_gate, w_up, w_down, final_norm):
    b, t, _ = x.shape
    assert mix_norm.shape[0] == 1, "single-layer block"
    assert t % CHUNK == 0 and t // CHUNK >= 3

    w = w_in[0].astype(BF16)
    na_q, na_k, na_v, df_q, df_k, df_v, g_na, g_df = jnp.split(
        w, [512, 1024, 1536, 2048, 2560, 3072, 4096], axis=1)
    w_t = jnp.concatenate([na_q, na_v, df_q, df_v], axis=1).T
    w_s = jnp.concatenate([na_k, df_k, g_na, g_df], axis=1)
    gain = mix_norm[0][None].astype(F32)

    pos = jnp.arange(N_META + t, dtype=jnp.int32)
    tables_meta = _rope_tables(pos[:N_META])
    tables_real = _rope_tables(pos[N_META:])

    na_qt, na_vt, df_qt, df_vt, na_k_r, df_k_r, g_na_r, g_df_r = _in_proj(
        x, gain, w_t, w_s, *tables_real, rows=CHUNK)
    meta = _in_proj(meta_tokens[None].astype(x.dtype), gain, w_t, w_s, *tables_meta, rows=N_META)
    na_vt_m, df_vt_m, na_k_m, df_k_m = meta[1][0], meta[3][0, 0], meta[4][0], meta[5][0]

    bias = _na_bias_tiles(na_rpb[0], t // GRID_W)
    na_out_t = _na_attention(na_qt, na_k_r, na_vt, na_k_m, na_vt_m, bias)

    lam_vecs = jnp.stack([lambda_q1[0], lambda_k1[0], lambda_q2[0], lambda_k2[0]]).astype(F32)
    df_out_t = _df_attention(df_qt, df_k_r, df_vt, df_k_m, df_vt_m, lam_vecs,
                             diff_subln[0].astype(F32)[:, None])

    return _out_ffn(x, na_out_t, df_out_t, g_na_r, g_df_r,
                    w_na_out[0].astype(BF16), w_diff_out[0].astype(BF16), w_o[0].astype(BF16),
                    ffn_norm[0][None].astype(F32), w_gate[0].astype(BF16), w_up[0].astype(BF16),
                    w_down[0].astype(BF16), final_norm[None].astype(F32), rows=512)
```

```python
import functools
import math

import jax
import jax.numpy as jnp
from jax import lax
from jax.experimental import pallas as pl
from jax.experimental.pallas import tpu as pltpu

D_MODEL = 1024
GRID_W = 64
N_META = 16
NA_HEADS = 8
NA_HEAD_DIM = 64
NA_WIN_ROWS = 8
NA_WIN_COLS = 16
DIFF_HEADS = 4
DIFF_HEAD_DIM = 64
NA_WIDTH = NA_HEADS * NA_HEAD_DIM
DIFF_WIDTH = DIFF_HEADS * 2 * DIFF_HEAD_DIM
D_FF = 2816
ROPE_THETA = 10000.0
NORM_EPS = 1e-6
SUBLN_EPS = 1e-5
LAMBDA_INIT = 0.8 - 0.6 * math.exp(-0.3 * 0)

CHUNK = 512
NA_GROUP_ROWS = CHUNK // GRID_W
NA_KEY_ROWS = 16
DF_KEYS = CHUNK
NA_PAIR = 2 * GRID_W
NA_PAIR_KEY_ROWS = NA_WIN_ROWS + 2
NA_QUAD = 4
NA_QUAD_DIM = NA_QUAD * NA_HEAD_DIM
V_ROWS = 128 + 16
LOG2_E = math.log2(math.e)
MASK_VALUE = -1e30
VMEM_LIMIT_BYTES = 56 * 1024 * 1024

BF16 = jnp.bfloat16
F32 = jnp.float32


def _dot(a, b):
    return jnp.dot(a, b, preferred_element_type=F32)


def _dot_nt(a, b):
    return lax.dot_general(a, b, (((1,), (1,)), ((), ())), preferred_element_type=F32)


def _dot_tn(a, b):
    return lax.dot_general(a, b, (((0,), (0,)), ((), ())), preferred_element_type=F32)


def _rms_scale(x, eps):
    return x * lax.rsqrt(jnp.mean(x * x, axis=-1, keepdims=True) + eps)


def _in_proj_kernel(x_ref, g_ref, wt_ref, ws_ref, cos_t_ref, sin_t_ref, cos_s_ref, sin_s_ref,
                    na_qt_ref, na_vt_ref, df_qt_ref, df_vt_ref, na_k_ref, df_k_ref,
                    g_na_ref, g_df_ref):
    h = (_rms_scale(x_ref[0], NORM_EPS) * g_ref[...]).astype(BF16)

    n_ch, chunk = na_qt_ref.shape[1], na_qt_ref.shape[-1]
    lanes = [slice(ci * chunk, (ci + 1) * chunk) for ci in range(n_ch)]
    nq = (_dot_nt(wt_ref[0:512], h) * (LOG2_E * NA_HEAD_DIM ** -0.5)).astype(BF16)
    for ci in range(n_ch):
        na_qt_ref[0, ci] = nq[:, lanes[ci]]
    nv = _dot_nt(wt_ref[512:1024], h).astype(BF16)
    dv = _dot_nt(wt_ref[1536:2048], h).astype(BF16)
    pad_row = lax.broadcasted_iota(jnp.int32, (V_ROWS - 128, chunk), 0)
    ones_pad = jnp.where(pad_row == 0, 1.0, 0.0).astype(BF16)
    for ci in range(n_ch):
        for hd in range(DIFF_HEADS):
            df_vt_ref[0, ci, hd, 0:128, :] = dv[hd * 128:(hd + 1) * 128, lanes[ci]]
            df_vt_ref[0, ci, hd, 128:V_ROWS, :] = ones_pad
    pair = na_vt_ref.shape[-1]
    pad_row = lax.broadcasted_iota(jnp.int32, (V_ROWS - 128, pair), 0)
    ones_pair = jnp.where(pad_row == 0, 1.0, 0.0).astype(BF16)
    for hp in range(NA_HEADS // 2):
        for pr in range(nv.shape[1] // pair):
            na_vt_ref[0, hp, pr, 0:128, :] = nv[hp * 128:(hp + 1) * 128, pr * pair:(pr + 1) * pair]
            na_vt_ref[0, hp, pr, 128:V_ROWS, :] = ones_pair
    dq = _dot_nt(wt_ref[1024:1536], h) * (LOG2_E * DIFF_HEAD_DIM ** -0.5)
    cos_t = cos_t_ref[...]
    sin_t = sin_t_ref[...]
    half = DIFF_HEAD_DIM // 2
    for c in range(DIFF_WIDTH // DIFF_HEAD_DIM):
        x1 = dq[c * 64:c * 64 + half]
        x2 = dq[c * 64 + half:(c + 1) * 64]
        top = (x1 * cos_t - x2 * sin_t).astype(BF16)
        bottom = (x2 * cos_t + x1 * sin_t).astype(BF16)
        for ci in range(n_ch):
            df_qt_ref[0, ci, c * 64:c * 64 + half, :] = top[:, lanes[ci]]
            df_qt_ref[0, ci, c * 64 + half:(c + 1) * 64, :] = bottom[:, lanes[ci]]

    na_k_ref[0] = _dot(h, ws_ref[:, 0:512]).astype(BF16)
    dk = _dot(h, ws_ref[:, 512:1024])
    cos_s = cos_s_ref[...]
    sin_s = sin_s_ref[...]
    lane = lax.broadcasted_iota(jnp.int32, cos_s.shape, 1)
    low_half = (lane % DIFF_HEAD_DIM) < half
    for c in range(DIFF_WIDTH // 128):
        xk = dk[:, c * 128:(c + 1) * 128]
        partner = jnp.where(low_half, pltpu.roll(xk, 128 - half, 1), pltpu.roll(xk, half, 1))
        df_k_ref[0, :, c * 128:(c + 1) * 128] = (xk * cos_s + partner * sin_s).astype(BF16)
    g_na_ref[0] = jax.nn.sigmoid(_dot(h, ws_ref[:, 1024:2048])).astype(BF16)
    g_df_ref[0] = jax.nn.sigmoid(_dot(h, ws_ref[:, 2048:3072])).astype(BF16)


def _in_proj(x, gain, w_t, w_s, cos_t, sin_t, cos_s, sin_s, rows):
    b, t, _ = x.shape
    n = t // rows
    const = lambda shape: pl.BlockSpec(shape, lambda i, j: (0,) * len(shape),
                                       pipeline_mode=pl.Buffered(1))
    chunk = min(CHUNK, rows)
    n_ch = rows // chunk
    feat_major = jax.ShapeDtypeStruct((b, t // chunk, 512, chunk), BF16)
    feat_spec = pl.BlockSpec((1, n_ch, 512, chunk), lambda i, j: (i, j, 0, 0))
    pair = min(NA_PAIR, rows)
    value_major = jax.ShapeDtypeStruct((b, t // chunk, 4, V_ROWS, chunk), BF16)
    value_spec = pl.BlockSpec((1, n_ch, 4, V_ROWS, chunk), lambda i, j: (i, j, 0, 0, 0))
    tok = lambda width: jax.ShapeDtypeStruct((b, t, width), BF16)
    tok_spec = lambda width: pl.BlockSpec((1, rows, width), lambda i, j: (i, j, 0))
    return pl.pallas_call(
        _in_proj_kernel,
        grid=(b, n),
        in_specs=[
            pl.BlockSpec((1, rows, D_MODEL), lambda i, j: (i, j, 0)),
            const((1, D_MODEL)),
            const((2048, D_MODEL)),
            const((D_MODEL, 3072)),
            pl.BlockSpec((DIFF_HEAD_DIM // 2, rows), lambda i, j: (0, j)),
            pl.BlockSpec((DIFF_HEAD_DIM // 2, rows), lambda i, j: (0, j)),
            pl.BlockSpec((rows, 128), lambda i, j: (j, 0)),
            pl.BlockSpec((rows, 128), lambda i, j: (j, 0)),
        ],
        out_specs=[feat_spec,
                   pl.BlockSpec((1, NA_HEADS // 2, rows // pair, V_ROWS, pair),
                                lambda i, j: (i, 0, j, 0, 0)),
                   feat_spec, value_spec,
                   tok_spec(512), tok_spec(512), tok_spec(D_MODEL), tok_spec(D_MODEL)],
        out_shape=[feat_major,
                   jax.ShapeDtypeStruct((b, NA_HEADS // 2, t // pair, V_ROWS, pair), BF16),
                   feat_major, value_major,
                   tok(512), tok(512), tok(D_MODEL), tok(D_MODEL)],
        compiler_params=pltpu.CompilerParams(
            dimension_semantics=("parallel", "parallel"), vmem_limit_bytes=VMEM_LIMIT_BYTES),
        name="in_proj",
    )(x, gain, w_t, w_s, cos_t, sin_t, cos_s, sin_s)


def _rope_tables(pos):
    half = DIFF_HEAD_DIM // 2
    inv = ROPE_THETA ** (-jnp.arange(half, dtype=F32) / half)
    ang = pos.astype(F32)[:, None] * inv[None, :]
    cos, sin = jnp.cos(ang), jnp.sin(ang)
    cos_s = jnp.tile(cos, (1, 128 // half))
    sin_s = jnp.tile(jnp.concatenate([-sin, sin], axis=1), (1, 128 // DIFF_HEAD_DIM))
    return cos.T, sin.T, cos_s, sin_s


def _na_pair_base(first_row, n_rows):
    return jnp.clip(first_row - NA_WIN_ROWS // 2, 0, n_rows - NA_PAIR_KEY_ROWS)


def _na_pair_variants(n_rows):
    assert n_rows >= 2 * NA_PAIR_KEY_ROWS + 2
    kinds = []
    for first_row in (2 * NA_WIN_ROWS, 0, 2, n_rows - 4, n_rows - 2):
        base = min(max(first_row - NA_WIN_ROWS // 2, 0), n_rows - NA_PAIR_KEY_ROWS)
        rows = []
        for r in (first_row, first_row + 1):
            start = min(max(r - NA_WIN_ROWS // 2, 0), n_rows - NA_WIN_ROWS)
            rows.append((start - base, start - base + NA_WIN_ROWS, base - r))
        kinds.append(rows)
    return kinds


def _na_pair_kind(pair_index, n_pairs):
    return jnp.where(pair_index == 0, 1, jnp.where(pair_index == 1, 2, jnp.where(
        pair_index == n_pairs - 2, 3, jnp.where(pair_index == n_pairs - 1, 4, 0))))


def _na_bias_tiles(rpb, n_rows):
    n_heads, n_dr, n_dc = rpb.shape
    kinds = _na_pair_variants(n_rows)
    mid = NA_WIN_COLS - 1
    ring = jnp.concatenate([rpb[:, :, mid::-1].astype(F32),
                            jnp.zeros((n_heads, n_dr, 128 - n_dc), F32),
                            rpb[:, :, :mid:-1].astype(F32)], axis=-1)

    def body(ring_ref, o_ref):
        key_col = lax.broadcasted_iota(jnp.int32, (GRID_W, 128), 0)
        lane = lax.broadcasted_iota(jnp.int32, (GRID_W, 128), 1)
        col_start = jnp.clip(lane % GRID_W - NA_WIN_COLS // 2, 0, GRID_W - NA_WIN_COLS)
        col_ok = (key_col >= col_start) & (key_col < col_start + NA_WIN_COLS)
        masked = jnp.full((GRID_W, 128), MASK_VALUE, F32)
        tables = []
        for d in range(n_dr):
            x = jnp.broadcast_to(ring_ref[0, d:d + 1, :], (GRID_W, 128)) * LOG2_E
            for bit in range(GRID_W.bit_length() - 1):
                x = jnp.where((key_col >> bit) & 1 == 1, pltpu.roll(x, 1 << bit, 1), x)
            x = jnp.where(lane < GRID_W, x, pltpu.roll(x, GRID_W, 1))
            tables.append(jnp.where(col_ok, x, masked))
        for v, kind in enumerate(kinds):
            for i in range(NA_PAIR_KEY_ROWS):
                halves = [tables[shift + i + NA_WIN_ROWS - 1] if lo <= i < hi else masked
                          for lo, hi, shift in kind]
                o_ref[v, 0, i * GRID_W:(i + 1) * GRID_W, :] = jnp.where(
                    lane < GRID_W, halves[0], halves[1])

    n_keys = NA_PAIR_KEY_ROWS * GRID_W
    return pl.pallas_call(
        body,
        grid=(n_heads,),
        in_specs=[pl.BlockSpec((1, n_dr, 128), lambda h: (h, 0, 0))],
        out_specs=pl.BlockSpec((len(kinds), 1, n_keys, NA_PAIR),
                               lambda h: (0, h // NA_QUAD, 0, h % NA_QUAD)),
        out_shape=jax.ShapeDtypeStruct((len(kinds), n_heads // NA_QUAD, n_keys, NA_QUAD * NA_PAIR),
                                       F32),
        compiler_params=pltpu.CompilerParams(
            dimension_semantics=("parallel",), vmem_limit_bytes=VMEM_LIMIT_BYTES),
        name="na_bias",
    )(ring)


def _na_window_row(g, n_rows):
    quarter = NA_KEY_ROWS // 4
    return jnp.clip(2 * g - 1, 0, (n_rows - NA_KEY_ROWS) // quarter) * quarter


def _na_kernel(qt_ref, k_ref, vt_ref, km_ref, vm_ref, bias_ref, o_ref, *s_refs, n_rows):
    g = pl.program_id(1)
    n_pairs = n_rows // 2
    window_row = _na_window_row(g, n_rows)
    n_keys = NA_PAIR_KEY_ROWS * GRID_W
    head_of_row = lax.broadcasted_iota(jnp.int32, (NA_QUAD_DIM, NA_PAIR), 0) // NA_HEAD_DIM
    n_quads = NA_HEADS // NA_QUAD
    units = [(jp, quad) for jp in range(NA_GROUP_ROWS // 2) for quad in range(n_quads)]

    def local_row(jp):
        return _na_pair_base(2 * (g * (NA_GROUP_ROWS // 2) + jp), n_rows) - window_row

    def scores(u):
        jp, quad = units[u]
        kind = _na_pair_kind(g * (NA_GROUP_ROWS // 2) + jp, n_pairs)
        key_off = pl.multiple_of(local_row(jp) * GRID_W, NA_PAIR)
        feat = slice(quad * NA_QUAD_DIM, (quad + 1) * NA_QUAD_DIM)
        q_pair = qt_ref[0, 0, feat, jp * NA_PAIR:(jp + 1) * NA_PAIR]
        zero = jnp.zeros_like(q_pair)
        w = jnp.concatenate([jnp.where(head_of_row == hd, q_pair, zero) for hd in range(NA_QUAD)],
                            axis=1)
        s = _dot(k_ref[0, pl.ds(key_off, n_keys), feat], w) + bias_ref[kind, quad]
        s_refs[u][...] = s
        s_meta = _dot(km_ref[:, feat], w)
        m = jnp.maximum(jnp.max(_col_part(s, jnp.max), axis=0, keepdims=True),
                        jnp.max(s_meta, axis=0, keepdims=True))
        return s_meta, m

    def values(u, s_meta, m):
        jp, quad = units[u]
        slab_off = local_row(jp) // 2
        p = jnp.exp2((s_refs[u][...] - m).astype(BF16))
        for hp in range(NA_QUAD // 2):
            slab = (NA_QUAD // 2) * quad + hp
            vt = jnp.concatenate([vt_ref[0, slab, slab_off + i] for i in range(NA_PAIR_KEY_ROWS // 2)],
                                 axis=1)
            cols = slice(2 * hp * NA_PAIR, 2 * (hp + 1) * NA_PAIR)
            p_meta = jnp.exp2((s_meta[:, cols] - m[:, cols]).astype(BF16))
            o = _dot(vt, p[:, cols]) + _dot(vm_ref[slab, 0], p_meta)
            denom = o[128:129]
            for hd in range(2):
                first = quad * NA_QUAD_DIM + (2 * hp + hd) * NA_HEAD_DIM
                o_ref[0, 0, first:first + NA_HEAD_DIM, jp * NA_PAIR:(jp + 1) * NA_PAIR] = (
                    o[hd * NA_HEAD_DIM:(hd + 1) * NA_HEAD_DIM, hd * NA_PAIR:(hd + 1) * NA_PAIR]
                    / denom[:, hd * NA_PAIR:(hd + 1) * NA_PAIR]).astype(BF16)

    pending = scores(0)
    for u in range(len(units)):
        following = scores(u + 1) if u + 1 < len(units) else None
        values(u, *pending)
        pending = following


def _na_attention(qt, k, vt, k_meta, vt_meta, bias):
    b, n_groups, _, _ = qt.shape
    n_rows = n_groups * NA_GROUP_ROWS
    n_quads = NA_HEADS // NA_QUAD
    n_kinds, _, n_keys, _ = bias.shape
    return pl.pallas_call(
        functools.partial(_na_kernel, n_rows=n_rows),
        grid=(b, n_groups),
        in_specs=[
            pl.BlockSpec((1, 1, NA_WIDTH, CHUNK), lambda bi, g: (bi, g, 0, 0)),
            pl.BlockSpec((pl.Element(1), pl.Element(NA_KEY_ROWS * GRID_W), pl.Element(NA_WIDTH)),
                         lambda bi, g: (bi, _na_window_row(g, n_rows) * GRID_W, 0)),
            pl.BlockSpec((pl.Element(1), pl.Element(NA_HEADS // 2), pl.Element(NA_KEY_ROWS // 2),
                          pl.Element(V_ROWS), pl.Element(NA_PAIR)),
                         lambda bi, g: (bi, 0, _na_window_row(g, n_rows) // 2, 0, 0)),
            pl.BlockSpec((N_META, NA_WIDTH), lambda bi, g: (0, 0)),
            pl.BlockSpec((NA_HEADS // 2, 1, V_ROWS, N_META), lambda bi, g: (0, 0, 0, 0)),
            pl.BlockSpec((n_kinds, n_quads, n_keys, NA_QUAD * NA_PAIR), lambda bi, g: (0, 0, 0, 0),
                         pipeline_mode=pl.Buffered(1)),
        ],
        out_specs=pl.BlockSpec((1, 1, NA_WIDTH, CHUNK), lambda bi, g: (bi, g, 0, 0)),
        out_shape=jax.ShapeDtypeStruct(qt.shape, BF16),
        scratch_shapes=[pltpu.VMEM((n_keys, NA_QUAD * NA_PAIR), F32)]
        * (n_quads * NA_GROUP_ROWS // 2),
        compiler_params=pltpu.CompilerParams(
            dimension_semantics=("parallel", "parallel"), vmem_limit_bytes=VMEM_LIMIT_BYTES),
        name="na_attn",
    )(qt, k, vt, k_meta, vt_meta, bias)


def _col_part(x, op):
    return op(x.reshape(x.shape[0] // 8, 8, x.shape[1]), axis=0)


def _df_kernel(qt_ref, k_ref, vt_ref, km_ref, vm_ref, lam_ref, g_ref, o_ref,
               s_a0, s_a1, s_b0, s_b1, p_a0, p_a1, p_b0, p_b1, acc0_ref, acc1_ref, *, n_chunks, n_kc):
    s_even, s_odd = (s_a0, s_a1), (s_b0, s_b1)
    p_even, p_odd = (p_a0, p_a1), (p_b0, p_b1)
    acc_refs = (acc0_ref, acc1_ref)
    n_q = n_chunks
    k_meta = km_ref[...]
    vt_meta = vm_ref[0]
    lam_v = lam_ref[...]
    lam = (jnp.exp(jnp.sum(lam_v[0:1] * lam_v[1:2], axis=1, keepdims=True))
           - jnp.exp(jnp.sum(lam_v[2:3] * lam_v[3:4], axis=1, keepdims=True)) + LAMBDA_INIT)

    def masked_q(qb):
        qt = qt_ref[0, qb]
        row = lax.broadcasted_iota(jnp.int32, qt.shape, 0)
        zero = jnp.zeros_like(qt)
        return (jnp.where(row < DIFF_HEAD_DIM, qt, zero), jnp.where(row >= DIFF_HEAD_DIM, qt, zero))

    def qk(qts, chunk, s_bufs):
        kc = k_ref[0, chunk]
        part = []
        for c in range(2):
            s = _dot(kc, qts[c])
            s_bufs[c][...] = s
            part.append(_col_part(s, jnp.max))
        return part

    def new_max(m, part):
        m_new = [jnp.maximum(m[c], jnp.max(part[c], axis=0, keepdims=True)) for c in range(2)]
        return m_new, [jnp.exp2(m[c] - m_new[c]) for c in range(2)]

    def expo(s_bufs, p_bufs, m):
        for c in range(2):
            p_bufs[c][...] = jnp.exp2((s_bufs[c][...] - m[c]).astype(BF16))

    def pv(p_bufs, chunk, alpha):
        slabs = DF_KEYS // CHUNK
        for c in range(2):
            update = None
            for u in range(slabs):
                d = _dot(vt_ref[0, chunk * slabs + u, 0], p_bufs[c][u * CHUNK:(u + 1) * CHUNK, :])
                update = d if u == 0 else update + d
            acc_refs[c][...] = acc_refs[c][...] * alpha[c] + update

    def finish(qb):
        o = (acc0_ref[0:128] / acc0_ref[128:129]
             - lam * (acc1_ref[0:128] / acc1_ref[128:129]))
        o = o * lax.rsqrt(jnp.mean(o * o, axis=0, keepdims=True) + SUBLN_EPS)
        o_ref[0, qb] = (o * g_ref[...] * (1.0 - LAMBDA_INIT)).astype(BF16)

    def handover(qb, cur):
        nxt = jnp.minimum(qb + 1, n_q - 1)
        qts = masked_q(nxt)
        part0 = qk(qts, 0, s_even)
        s_meta = [_dot(k_meta, qts[c]) for c in range(2)]
        if cur is not None:
            expo(s_odd, p_odd, cur[0])
            pv(p_even, n_kc - 2, cur[2])
        m0 = [jnp.maximum(jnp.max(part0[c], axis=0, keepdims=True),
                          jnp.max(s_meta[c], axis=0, keepdims=True)) for c in range(2)]
        part1 = qk(qts, 1, s_odd)
        expo(s_even, p_even, m0)
        if cur is not None:
            pv(p_odd, n_kc - 1, cur[1])
            finish(qb)
        for c in range(2):
            acc_refs[c][...] = _dot(vt_meta, jnp.exp2((s_meta[c] - m0[c]).astype(BF16)))
        m1, a1 = new_max(m0, part1)
        return tuple(m1), tuple(a1), tuple(jnp.ones_like(x) for x in m1)

    def steps(qts, t, m_e, a_e, a_p):
        part = qk(qts, t + 2, s_even)
        expo(s_odd, p_odd, m_e)
        pv(p_even, t, a_p)
        m_2, a_2 = new_max(m_e, part)
        part = qk(qts, t + 3, s_odd)
        expo(s_even, p_even, m_2)
        pv(p_odd, t + 1, a_e)
        m_3, a_3 = new_max(m_2, part)
        return tuple(m_3), tuple(a_3), tuple(a_2)

    def block(qb, carry):
        qts = masked_q(qb)
        for i in range(n_kc // 2 - 1):
            carry = steps(qts, 2 * i, *carry)
        return handover(qb, carry)

    assert n_kc % 2 == 0 and n_kc >= 4
    lax.fori_loop(0, n_q, block, handover(-1, None))


def _df_attention(qt, k, vt, k_meta, vt_meta, lam_vecs, subln_col):
    b, n_chunks, _, _ = qt.shape
    n_kc = n_chunks * CHUNK // DF_KEYS
    k4 = k.reshape(b, n_kc, DF_KEYS, DIFF_WIDTH)
    return pl.pallas_call(
        functools.partial(_df_kernel, n_chunks=n_chunks, n_kc=n_kc),
        grid=(b, DIFF_HEADS),
        in_specs=[
            pl.BlockSpec((1, n_chunks, 128, CHUNK), lambda bi, h: (bi, 0, h, 0)),
            pl.BlockSpec((1, n_kc, DF_KEYS, 128), lambda bi, h: (bi, 0, 0, h)),
            pl.BlockSpec((1, n_chunks, 1, V_ROWS, CHUNK), lambda bi, h: (bi, 0, h, 0, 0)),
            pl.BlockSpec((N_META, 128), lambda bi, h: (0, h)),
            pl.BlockSpec((1, V_ROWS, N_META), lambda bi, h: (h, 0, 0)),
            pl.BlockSpec((4, DIFF_HEAD_DIM), lambda bi, h: (0, 0)),
            pl.BlockSpec((128, 1), lambda bi, h: (0, 0)),
        ],
        out_specs=pl.BlockSpec((1, n_chunks, 128, CHUNK), lambda bi, h: (bi, 0, h, 0)),
        out_shape=jax.ShapeDtypeStruct(qt.shape, BF16),
        scratch_shapes=[pltpu.VMEM((DF_KEYS, CHUNK), F32)] * 4
        + [pltpu.VMEM((DF_KEYS, CHUNK), BF16)] * 4
        + [pltpu.VMEM((V_ROWS, CHUNK), F32)] * 2,
        compiler_params=pltpu.CompilerParams(
            dimension_semantics=("parallel", "parallel"), vmem_limit_bytes=VMEM_LIMIT_BYTES),
        name="df_attn",
    )(qt, k4, vt, k_meta, vt_meta, lam_vecs, subln_col)


def _out_ffn_kernel(x_ref, nat_ref, dft_ref, g_na_ref, g_df_ref, w_na_ref, w_df_ref, w_o_ref,
                    ffn_g_ref, w_gate_ref, w_up_ref, w_down_ref, fin_g_ref, o_ref):
    o_na = _dot_tn(nat_ref[0, 0], w_na_ref[...])
    o_df = _dot_tn(dft_ref[0, 0], w_df_ref[...])
    merged = g_na_ref[0].astype(F32) * o_na + g_df_ref[0].astype(F32) * o_df
    x1 = x_ref[0] + _dot(merged.astype(BF16), w_o_ref[...])
    h = (_rms_scale(x1, NORM_EPS) * ffn_g_ref[...]).astype(BF16)
    gate = _dot(h, w_gate_ref[...])
    up = _dot(h, w_up_ref[...])
    act = (gate * jax.nn.sigmoid(gate) * up).astype(BF16)
    x2 = x1 + _dot(act, w_down_ref[...])
    o_ref[0] = _rms_scale(x2, NORM_EPS) * fin_g_ref[...]


def _out_ffn(x, nat, dft, g_na, g_df, w_na, w_df, w_o, ffn_g, w_gate, w_up, w_down, fin_g, rows):
    b, t, _ = x.shape
    per_chunk = CHUNK // rows
    const = lambda shape: pl.BlockSpec(shape, lambda i, j: (0,) * len(shape),
                                       pipeline_mode=pl.Buffered(1))
    tok_spec = pl.BlockSpec((1, rows, D_MODEL), lambda i, j: (i, j, 0))
    feat_spec = pl.BlockSpec((1, 1, 512, rows), lambda i, j: (i, j // per_chunk, 0, j % per_chunk))
    return pl.pallas_call(
        _out_ffn_kernel,
        grid=(b, t // rows),
        in_specs=[tok_spec, feat_spec, feat_spec, tok_spec, tok_spec,
                  const((NA_WIDTH, D_MODEL)), const((DIFF_WIDTH, D_MODEL)), const((D_MODEL, D_MODEL)),
                  const((1, D_MODEL)), const((D_MODEL, D_FF)), const((D_MODEL, D_FF)),
                  const((D_FF, D_MODEL)), const((1, D_MODEL))],
        out_specs=tok_spec,
        out_shape=jax.ShapeDtypeStruct(x.shape, F32),
        compiler_params=pltpu.CompilerParams(
            dimension_semantics=("parallel", "parallel"), vmem_limit_bytes=VMEM_LIMIT_BYTES),
        name="out_ffn",
    )(x, nat, dft, g_na, g_df, w_na, w_df, w_o, ffn_g, w_gate, w_up, w_down, fin_g)


def kernel(x, meta_tokens, mix_norm, w_in, na_rpb, lambda_q1, lambda_k1, lambda_q2, lambda_k2,
           diff_subln, w_na_out, w_diff_out, w_o, ffn_norm, w_gate, w_up, w_down, final_norm):
    b, t, _ = x.shape
    assert mix_norm.shape[0] == 1, "single-layer block"
    assert t % CHUNK == 0 and t // CHUNK >= 3

    w = w_in[0].astype(BF16)
    na_q, na_k, na_v, df_q, df_k, df_v, g_na, g_df = jnp.split(
        w, [512, 1024, 1536, 2048, 2560, 3072, 4096], axis=1)
    w_t = jnp.concatenate([na_q, na_v, df_q, df_v], axis=1).T
    w_s = jnp.concatenate([na_k, df_k, g_na, g_df], axis=1)
    gain = mix_norm[0][None].astype(F32)

    pos = jnp.arange(N_META + t, dtype=jnp.int32)
    tables_meta = _rope_tables(pos[:N_META])
    tables_real = _rope_tables(pos[N_META:])

    na_qt, na_vt, df_qt, df_vt, na_k_r, df_k_r, g_na_r, g_df_r = _in_proj(
        x, gain, w_t, w_s, *tables_real, rows=2 * CHUNK)
    meta = _in_proj(meta_tokens[None].astype(x.dtype), gain, w_t, w_s, *tables_meta, rows=N_META)
    na_vt_m, df_vt_m, na_k_m, df_k_m = meta[1][0], meta[3][0, 0], meta[4][0], meta[5][0]

    bias = _na_bias_tiles(na_rpb[0], t // GRID_W)
    na_out_t = _na_attention(na_qt, na_k_r, na_vt, na_k_m, na_vt_m, bias)

    lam_vecs = jnp.stack([lambda_q1[0], lambda_k1[0], lambda_q2[0], lambda_k2[0]]).astype(F32)
    df_out_t = _df_attention(df_qt, df_k_r, df_vt, df_k_m, df_vt_m, lam_vecs,
                             diff_subln[0].astype(F32)[:, None])

    return _out_ffn(x, na_out_t, df_out_t, g_na_r, g_df_r,
                    w_na_out[0].astype(BF16), w_diff_out[0].astype(BF16), w_o[0].astype(BF16),
                    ffn_norm[0][None].astype(F32), w_gate[0].astype(BF16), w_up[0].astype(BF16),
                    w_down[0].astype(BF16), final_norm[None].astype(F32), rows=512)
```

```python
import functools
import math

import jax
import jax.numpy as jnp
from jax import lax
from jax.experimental import pallas as pl
from jax.experimental.pallas import tpu as pltpu

D_MODEL = 1024
GRID_W = 64
N_META = 16
NA_HEADS = 8
NA_HEAD_DIM = 64
NA_WIN_ROWS = 8
NA_WIN_COLS = 16
DIFF_HEADS = 4
DIFF_HEAD_DIM = 64
NA_WIDTH = NA_HEADS * NA_HEAD_DIM
DIFF_WIDTH = DIFF_HEADS * 2 * DIFF_HEAD_DIM
D_FF = 2816
ROPE_THETA = 10000.0
NORM_EPS = 1e-6
SUBLN_EPS = 1e-5
LAMBDA_INIT = 0.8 - 0.6 * math.exp(-0.3 * 0)

CHUNK = 512
NA_GROUP_ROWS = CHUNK // GRID_W
NA_KEY_ROWS = 16
NA_PAIR = 2 * GRID_W
NA_PAIR_KEY_ROWS = NA_WIN_ROWS + 2
NA_QUAD = 4
NA_QUAD_DIM = NA_QUAD * NA_HEAD_DIM
V_ROWS = 128 + 16
LOG2_E = math.log2(math.e)
MASK_VALUE = -1e30
VMEM_LIMIT_BYTES = 56 * 1024 * 1024

BF16 = jnp.bfloat16
F32 = jnp.float32


def _dot(a, b):
    return jnp.dot(a, b, preferred_element_type=F32)


def _dot_nt(a, b):
    return lax.dot_general(a, b, (((1,), (1,)), ((), ())), preferred_element_type=F32)


def _dot_tn(a, b):
    return lax.dot_general(a, b, (((0,), (0,)), ((), ())), preferred_element_type=F32)


def _rms_scale(x, eps):
    return x * lax.rsqrt(jnp.mean(x * x, axis=-1, keepdims=True) + eps)


def _in_proj_kernel(x_ref, g_ref, wt_ref, ws_ref, cos_t_ref, sin_t_ref, cos_s_ref, sin_s_ref,
                    na_qt_ref, na_vt_ref, df_qt_ref, df_vt_ref, na_k_ref, df_k_ref,
                    g_na_ref, g_df_ref):
    h = (_rms_scale(x_ref[0], NORM_EPS) * g_ref[...]).astype(BF16)

    n_ch, chunk = na_qt_ref.shape[1], na_qt_ref.shape[-1]
    lanes = [slice(ci * chunk, (ci + 1) * chunk) for ci in range(n_ch)]
    nq = (_dot_nt(wt_ref[0:512], h) * (LOG2_E * NA_HEAD_DIM ** -0.5)).astype(BF16)
    for ci in range(n_ch):
        na_qt_ref[0, ci] = nq[:, lanes[ci]]
    nv = _dot_nt(wt_ref[512:1024], h).astype(BF16)
    dv = _dot_nt(wt_ref[1536:2048], h).astype(BF16)
    for ci in range(n_ch):
        df_vt_ref[0, ci] = dv[:, lanes[ci]]
    pair = na_vt_ref.shape[-1]
    pad_row = lax.broadcasted_iota(jnp.int32, (V_ROWS - 128, pair), 0)
    ones_pair = jnp.where(pad_row == 0, 1.0, 0.0).astype(BF16)
    for hp in range(NA_HEADS // 2):
        for pr in range(nv.shape[1] // pair):
            na_vt_ref[0, hp, pr, 0:128, :] = nv[hp * 128:(hp + 1) * 128, pr * pair:(pr + 1) * pair]
            na_vt_ref[0, hp, pr, 128:V_ROWS, :] = ones_pair
    dq = _dot_nt(wt_ref[1024:1536], h) * (LOG2_E * DIFF_HEAD_DIM ** -0.5)
    cos_t = cos_t_ref[...]
    sin_t = sin_t_ref[...]
    half = DIFF_HEAD_DIM // 2
    for c in range(DIFF_WIDTH // DIFF_HEAD_DIM):
        x1 = dq[c * 64:c * 64 + half]
        x2 = dq[c * 64 + half:(c + 1) * 64]
        top = (x1 * cos_t - x2 * sin_t).astype(BF16)
        bottom = (x2 * cos_t + x1 * sin_t).astype(BF16)
        for ci in range(n_ch):
            df_qt_ref[0, ci, c * 64:c * 64 + half, :] = top[:, lanes[ci]]
            df_qt_ref[0, ci, c * 64 + half:(c + 1) * 64, :] = bottom[:, lanes[ci]]

    na_k_ref[0] = _dot(h, ws_ref[:, 0:512]).astype(BF16)
    dk = _dot(h, ws_ref[:, 512:1024])
    cos_s = cos_s_ref[...]
    sin_s = sin_s_ref[...]
    lane = lax.broadcasted_iota(jnp.int32, cos_s.shape, 1)
    low_half = (lane % DIFF_HEAD_DIM) < half
    for c in range(DIFF_WIDTH // 128):
        xk = dk[:, c * 128:(c + 1) * 128]
        partner = jnp.where(low_half, pltpu.roll(xk, 128 - half, 1), pltpu.roll(xk, half, 1))
        df_k_ref[0, :, c * 128:(c + 1) * 128] = (xk * cos_s + partner * sin_s).astype(BF16)
    g_na_ref[0] = jax.nn.sigmoid(_dot(h, ws_ref[:, 1024:2048])).astype(BF16)
    g_df_ref[0] = jax.nn.sigmoid(_dot(h, ws_ref[:, 2048:3072])).astype(BF16)


def _in_proj(x, gain, w_t, w_s, cos_t, sin_t, cos_s, sin_s, rows):
    b, t, _ = x.shape
    n = t // rows
    const = lambda shape: pl.BlockSpec(shape, lambda i, j: (0,) * len(shape),
                                       pipeline_mode=pl.Buffered(1))
    chunk = min(CHUNK, rows)
    n_ch = rows // chunk
    feat_major = jax.ShapeDtypeStruct((b, t // chunk, 512, chunk), BF16)
    feat_spec = pl.BlockSpec((1, n_ch, 512, chunk), lambda i, j: (i, j, 0, 0))
    pair = min(NA_PAIR, rows)
    tok = lambda width: jax.ShapeDtypeStruct((b, t, width), BF16)
    tok_spec = lambda width: pl.BlockSpec((1, rows, width), lambda i, j: (i, j, 0))
    return pl.pallas_call(
        _in_proj_kernel,
        grid=(b, n),
        in_specs=[
            pl.BlockSpec((1, rows, D_MODEL), lambda i, j: (i, j, 0)),
            const((1, D_MODEL)),
            const((2048, D_MODEL)),
            const((D_MODEL, 3072)),
            pl.BlockSpec((DIFF_HEAD_DIM // 2, rows), lambda i, j: (0, j)),
            pl.BlockSpec((DIFF_HEAD_DIM // 2, rows), lambda i, j: (0, j)),
            pl.BlockSpec((rows, 128), lambda i, j: (j, 0)),
            pl.BlockSpec((rows, 128), lambda i, j: (j, 0)),
        ],
        out_specs=[feat_spec,
                   pl.BlockSpec((1, NA_HEADS // 2, rows // pair, V_ROWS, pair),
                                lambda i, j: (i, 0, j, 0, 0)),
                   feat_spec, feat_spec,
                   tok_spec(512), tok_spec(512), tok_spec(D_MODEL), tok_spec(D_MODEL)],
        out_shape=[feat_major,
                   jax.ShapeDtypeStruct((b, NA_HEADS // 2, t // pair, V_ROWS, pair), BF16),
                   feat_major, feat_major,
                   tok(512), tok(512), tok(D_MODEL), tok(D_MODEL)],
        compiler_params=pltpu.CompilerParams(
            dimension_semantics=("parallel", "parallel"), vmem_limit_bytes=VMEM_LIMIT_BYTES),
        name="in_proj",
    )(x, gain, w_t, w_s, cos_t, sin_t, cos_s, sin_s)


def _rope_tables(pos):
    half = DIFF_HEAD_DIM // 2
    inv = ROPE_THETA ** (-jnp.arange(half, dtype=F32) / half)
    ang = pos.astype(F32)[:, None] * inv[None, :]
    cos, sin = jnp.cos(ang), jnp.sin(ang)
    cos_s = jnp.tile(cos, (1, 128 // half))
    sin_s = jnp.tile(jnp.concatenate([-sin, sin], axis=1), (1, 128 // DIFF_HEAD_DIM))
    return cos.T, sin.T, cos_s, sin_s


def _na_pair_base(first_row, n_rows):
    return jnp.clip(first_row - NA_WIN_ROWS // 2, 0, n_rows - NA_PAIR_KEY_ROWS)


def _na_pair_variants(n_rows):
    assert n_rows >= 2 * NA_PAIR_KEY_ROWS + 2
    kinds = []
    for first_row in (2 * NA_WIN_ROWS, 0, 2, n_rows - 4, n_rows - 2):
        base = min(max(first_row - NA_WIN_ROWS // 2, 0), n_rows - NA_PAIR_KEY_ROWS)
        rows = []
        for r in (first_row, first_row + 1):
            start = min(max(r - NA_WIN_ROWS // 2, 0), n_rows - NA_WIN_ROWS)
            rows.append((start - base, start - base + NA_WIN_ROWS, base - r))
        kinds.append(rows)
    return kinds


def _na_pair_kind(pair_index, n_pairs):
    return jnp.where(pair_index == 0, 1, jnp.where(pair_index == 1, 2, jnp.where(
        pair_index == n_pairs - 2, 3, jnp.where(pair_index == n_pairs - 1, 4, 0))))


def _na_bias_tiles(rpb, n_rows):
    n_heads, n_dr, n_dc = rpb.shape
    kinds = _na_pair_variants(n_rows)
    mid = NA_WIN_COLS - 1
    ring = jnp.concatenate([rpb[:, :, mid::-1].astype(F32),
                            jnp.zeros((n_heads, n_dr, 128 - n_dc), F32),
                            rpb[:, :, :mid:-1].astype(F32)], axis=-1)

    def body(ring_ref, o_ref):
        key_col = lax.broadcasted_iota(jnp.int32, (GRID_W, 128), 0)
        lane = lax.broadcasted_iota(jnp.int32, (GRID_W, 128), 1)
        col_start = jnp.clip(lane % GRID_W - NA_WIN_COLS // 2, 0, GRID_W - NA_WIN_COLS)
        col_ok = (key_col >= col_start) & (key_col < col_start + NA_WIN_COLS)
        masked = jnp.full((GRID_W, 128), MASK_VALUE, F32)
        tables = []
        for d in range(n_dr):
            x = jnp.broadcast_to(ring_ref[0, d:d + 1, :], (GRID_W, 128)) * LOG2_E
            for bit in range(GRID_W.bit_length() - 1):
                x = jnp.where((key_col >> bit) & 1 == 1, pltpu.roll(x, 1 << bit, 1), x)
            x = jnp.where(lane < GRID_W, x, pltpu.roll(x, GRID_W, 1))
            tables.append(jnp.where(col_ok, x, masked))
        for v, kind in enumerate(kinds):
            for i in range(NA_PAIR_KEY_ROWS):
                halves = [tables[shift + i + NA_WIN_ROWS - 1] if lo <= i < hi else masked
                          for lo, hi, shift in kind]
                o_ref[v, 0, i * GRID_W:(i + 1) * GRID_W, :] = jnp.where(
                    lane < GRID_W, halves[0], halves[1])

    n_keys = NA_PAIR_KEY_ROWS * GRID_W
    return pl.pallas_call(
        body,
        grid=(n_heads,),
        in_specs=[pl.BlockSpec((1, n_dr, 128), lambda h: (h, 0, 0))],
        out_specs=pl.BlockSpec((len(kinds), 1, n_keys, NA_PAIR),
                               lambda h: (0, h // NA_QUAD, 0, h % NA_QUAD)),
        out_shape=jax.ShapeDtypeStruct((len(kinds), n_heads // NA_QUAD, n_keys, NA_QUAD * NA_PAIR),
                                       F32),
        compiler_params=pltpu.CompilerParams(
            dimension_semantics=("parallel",), vmem_limit_bytes=VMEM_LIMIT_BYTES),
        name="na_bias",
    )(ring)


def _na_window_row(g, n_rows):
    quarter = NA_KEY_ROWS // 4
    return jnp.clip(2 * g - 1, 0, (n_rows - NA_KEY_ROWS) // quarter) * quarter


def _na_kernel(qt_ref, k_ref, vt_ref, km_ref, vm_ref, bias_ref, o_ref, *s_refs, n_rows):
    g = pl.program_id(1)
    n_pairs = n_rows // 2
    window_row = _na_window_row(g, n_rows)
    n_keys = NA_PAIR_KEY_ROWS * GRID_W
    head_of_row = lax.broadcasted_iota(jnp.int32, (NA_QUAD_DIM, NA_PAIR), 0) // NA_HEAD_DIM
    n_quads = NA_HEADS // NA_QUAD
    units = [(jp, quad) for jp in range(NA_GROUP_ROWS // 2) for quad in range(n_quads)]

    def local_row(jp):
        return _na_pair_base(2 * (g * (NA_GROUP_ROWS // 2) + jp), n_rows) - window_row

    def scores(u):
        jp, quad = units[u]
        kind = _na_pair_kind(g * (NA_GROUP_ROWS // 2) + jp, n_pairs)
        key_off = pl.multiple_of(local_row(jp) * GRID_W, NA_PAIR)
        feat = slice(quad * NA_QUAD_DIM, (quad + 1) * NA_QUAD_DIM)
        q_pair = qt_ref[0, 0, feat, jp * NA_PAIR:(jp + 1) * NA_PAIR]
        zero = jnp.zeros_like(q_pair)
        w = jnp.concatenate([jnp.where(head_of_row == hd, q_pair, zero) for hd in range(NA_QUAD)],
                            axis=1)
        s = _dot(k_ref[0, pl.ds(key_off, n_keys), feat], w) + bias_ref[kind, quad]
        s_refs[u][...] = s
        s_meta = _dot(km_ref[:, feat], w)
        m = jnp.maximum(jnp.max(_col_part(s, jnp.max), axis=0, keepdims=True),
                        jnp.max(s_meta, axis=0, keepdims=True))
        return s_meta, m

    def values(u, s_meta, m):
        jp, quad = units[u]
        slab_off = local_row(jp) // 2
        p = jnp.exp2((s_refs[u][...] - m).astype(BF16))
        for hp in range(NA_QUAD // 2):
            slab = (NA_QUAD // 2) * quad + hp
            vt = jnp.concatenate([vt_ref[0, slab, slab_off + i] for i in range(NA_PAIR_KEY_ROWS // 2)],
                                 axis=1)
            cols = slice(2 * hp * NA_PAIR, 2 * (hp + 1) * NA_PAIR)
            p_meta = jnp.exp2((s_meta[:, cols] - m[:, cols]).astype(BF16))
            o = _dot(vt, p[:, cols]) + _dot(vm_ref[slab, 0], p_meta)
            denom = o[128:129]
            for hd in range(2):
                first = quad * NA_QUAD_DIM + (2 * hp + hd) * NA_HEAD_DIM
                o_ref[0, 0, first:first + NA_HEAD_DIM, jp * NA_PAIR:(jp + 1) * NA_PAIR] = (
                    o[hd * NA_HEAD_DIM:(hd + 1) * NA_HEAD_DIM, hd * NA_PAIR:(hd + 1) * NA_PAIR]
                    / denom[:, hd * NA_PAIR:(hd + 1) * NA_PAIR]).astype(BF16)

    pending = scores(0)
    for u in range(len(units)):
        following = scores(u + 1) if u + 1 < len(units) else None
        values(u, *pending)
        pending = following


def _na_attention(qt, k, vt, k_meta, vt_meta, bias):
    b, n_groups, _, _ = qt.shape
    n_rows = n_groups * NA_GROUP_ROWS
    n_quads = NA_HEADS // NA_QUAD
    n_kinds, _, n_keys, _ = bias.shape
    return pl.pallas_call(
        functools.partial(_na_kernel, n_rows=n_rows),
        grid=(b, n_groups),
        in_specs=[
            pl.BlockSpec((1, 1, NA_WIDTH, CHUNK), lambda bi, g: (bi, g, 0, 0)),
            pl.BlockSpec((pl.Element(1), pl.Element(NA_KEY_ROWS * GRID_W), pl.Element(NA_WIDTH)),
                         lambda bi, g: (bi, _na_window_row(g, n_rows) * GRID_W, 0)),
            pl.BlockSpec((pl.Element(1), pl.Element(NA_HEADS // 2), pl.Element(NA_KEY_ROWS // 2),
                          pl.Element(V_ROWS), pl.Element(NA_PAIR)),
                         lambda bi, g: (bi, 0, _na_window_row(g, n_rows) // 2, 0, 0)),
            pl.BlockSpec((N_META, NA_WIDTH), lambda bi, g: (0, 0)),
            pl.BlockSpec((NA_HEADS // 2, 1, V_ROWS, N_META), lambda bi, g: (0, 0, 0, 0)),
            pl.BlockSpec((n_kinds, n_quads, n_keys, NA_QUAD * NA_PAIR), lambda bi, g: (0, 0, 0, 0),
                         pipeline_mode=pl.Buffered(1)),
        ],
        out_specs=pl.BlockSpec((1, 1, NA_WIDTH, CHUNK), lambda bi, g: (bi, g, 0, 0)),
        out_shape=jax.ShapeDtypeStruct(qt.shape, BF16),
        scratch_shapes=[pltpu.VMEM((n_keys, NA_QUAD * NA_PAIR), F32)]
        * (n_quads * NA_GROUP_ROWS // 2),
        compiler_params=pltpu.CompilerParams(
            dimension_semantics=("parallel", "parallel"), vmem_limit_bytes=VMEM_LIMIT_BYTES),
        name="na_attn",
    )(qt, k, vt, k_meta, vt_meta, bias)


def _col_part(x, op):
    return op(x.reshape(x.shape[0] // 8, 8, x.shape[1]), axis=0)


def _df_kernel(qt_ref, k_ref, vt_ref, km_ref, vm_ref, lam_ref, g_ref, o_ref,
               s_a0, s_a1, s_b0, s_b1, acc0_ref, acc1_ref, *, n_chunks):
    s_even, s_odd = (s_a0, s_a1), (s_b0, s_b1)
    acc_refs = (acc0_ref, acc1_ref)
    n_q = n_chunks
    k_meta = km_ref[...]
    vt_meta = vm_ref[...]
    lam_v = lam_ref[...]
    lam = (jnp.exp(jnp.sum(lam_v[0:1] * lam_v[1:2], axis=1, keepdims=True))
           - jnp.exp(jnp.sum(lam_v[2:3] * lam_v[3:4], axis=1, keepdims=True)) + LAMBDA_INIT)

    def masked_q(qb):
        qt = qt_ref[0, qb]
        row = lax.broadcasted_iota(jnp.int32, qt.shape, 0)
        zero = jnp.zeros_like(qt)
        return (jnp.where(row < DIFF_HEAD_DIM, qt, zero), jnp.where(row >= DIFF_HEAD_DIM, qt, zero))

    def stage(chunk, s_cur, s_nxt, m, alpha, l8, nxt):
        part, l8 = [], list(l8)
        for c in range(2):
            if nxt is not None:
                s_next = _dot(k_ref[0, nxt[0]], nxt[1][c])
                s_nxt[c][...] = s_next
                part.append(_col_part(s_next, jnp.max))
            p = jnp.exp2(s_cur[c][...] - m[c])
            l8[c] = l8[c] + _col_part(p, jnp.sum)
            acc_refs[c][...] = (acc_refs[c][...] * alpha[c]
                                + _dot(vt_ref[0, chunk], p.astype(BF16)))
        return part, l8

    def advance(m, l8, part):
        m_new = [jnp.maximum(m[c], jnp.max(part[c], axis=0, keepdims=True)) for c in range(2)]
        alpha = [jnp.exp2(m[c] - m_new[c]) for c in range(2)]
        return tuple(m_new), tuple(alpha), tuple(l8[c] * alpha[c] for c in range(2))

    def finish(qb, l8):
        l0 = jnp.sum(l8[0], axis=0, keepdims=True)
        l1 = jnp.sum(l8[1], axis=0, keepdims=True)
        o = acc0_ref[...] / l0 - lam * (acc1_ref[...] / l1)
        o = o * lax.rsqrt(jnp.mean(o * o, axis=0, keepdims=True) + SUBLN_EPS)
        o_ref[0, qb] = (o * g_ref[...] * (1.0 - LAMBDA_INIT)).astype(BF16)

    def handover(qb, cur):
        nxt = jnp.minimum(qb + 1, n_q - 1)
        qts = masked_q(nxt)
        s_meta = [_dot(k_meta, qts[c]) for c in range(2)]
        if cur is not None:
            part, l8 = stage(n_chunks - 1, s_odd, s_even, *cur, (0, qts))
            finish(qb, l8)
        else:
            part = []
            for c in range(2):
                s0 = _dot(k_ref[0, 0], qts[c])
                s_even[c][...] = s0
                part.append(_col_part(s0, jnp.max))
        m0, l0 = [], []
        for c in range(2):
            mc = jnp.maximum(jnp.max(part[c], axis=0, keepdims=True),
                             jnp.max(s_meta[c], axis=0, keepdims=True))
            p_meta = jnp.exp2(s_meta[c] - mc)
            acc_refs[c][...] = _dot(vt_meta, p_meta.astype(BF16))
            m0.append(mc)
            l0.append(_col_part(p_meta, jnp.sum))
        return tuple(m0), tuple(jnp.ones_like(x) for x in m0), tuple(l0)

    def block(qb, carry):
        qts = masked_q(qb)

        def pair(i, cr):
            part, l8 = stage(2 * i, s_even, s_odd, *cr, (2 * i + 1, qts))
            cr = advance(cr[0], l8, part)
            part, l8 = stage(2 * i + 1, s_odd, s_even, *cr, (2 * i + 2, qts))
            return advance(cr[0], l8, part)

        carry = lax.fori_loop(0, n_chunks // 2 - 1, pair, carry)
        part, l8 = stage(n_chunks - 2, s_even, s_odd, *carry, (n_chunks - 1, qts))
        carry = advance(carry[0], l8, part)
        return handover(qb, carry)

    assert n_chunks % 2 == 0 and n_chunks >= 4
    lax.fori_loop(0, n_q, block, handover(-1, None))


def _df_attention(qt, k, vt, k_meta, vt_meta, lam_vecs, subln_col):
    b, n_chunks, _, _ = qt.shape
    k4 = k.reshape(b, n_chunks, CHUNK, DIFF_WIDTH)
    return pl.pallas_call(
        functools.partial(_df_kernel, n_chunks=n_chunks),
        grid=(b, DIFF_HEADS),
        in_specs=[
            pl.BlockSpec((1, n_chunks, 128, CHUNK), lambda bi, h: (bi, 0, h, 0)),
            pl.BlockSpec((1, n_chunks, CHUNK, 128), lambda bi, h: (bi, 0, 0, h)),
            pl.BlockSpec((1, n_chunks, 128, CHUNK), lambda bi, h: (bi, 0, h, 0)),
            pl.BlockSpec((N_META, 128), lambda bi, h: (0, h)),
            pl.BlockSpec((128, N_META), lambda bi, h: (h, 0)),
            pl.BlockSpec((4, DIFF_HEAD_DIM), lambda bi, h: (0, 0)),
            pl.BlockSpec((128, 1), lambda bi, h: (0, 0)),
        ],
        out_specs=pl.BlockSpec((1, n_chunks, 128, CHUNK), lambda bi, h: (bi, 0, h, 0)),
        out_shape=jax.ShapeDtypeStruct(qt.shape, BF16),
        scratch_shapes=[pltpu.VMEM((CHUNK, CHUNK), F32)] * 4
        + [pltpu.VMEM((128, CHUNK), F32)] * 2,
        compiler_params=pltpu.CompilerParams(
            dimension_semantics=("parallel", "parallel"), vmem_limit_bytes=VMEM_LIMIT_BYTES),
        name="df_attn",
    )(qt, k4, vt, k_meta, vt_meta, lam_vecs, subln_col)


def _out_ffn_kernel(x_ref, nat_ref, dft_ref, g_na_ref, g_df_ref, w_na_ref, w_df_ref, w_o_ref,
                    ffn_g_ref, w_gate_ref, w_up_ref, w_down_ref, fin_g_ref, o_ref):
    o_na = _dot_tn(nat_ref[0, 0], w_na_ref[...])
    o_df = _dot_tn(dft_ref[0, 0], w_df_ref[...])
    merged = g_na_ref[0].astype(F32) * o_na + g_df_ref[0].astype(F32) * o_df
    x1 = x_ref[0] + _dot(merged.astype(BF16), w_o_ref[...])
    h = (_rms_scale(x1, NORM_EPS) * ffn_g_ref[...]).astype(BF16)
    gate = _dot(h, w_gate_ref[...])
    up = _dot(h, w_up_ref[...])
    act = (gate * jax.nn.sigmoid(gate) * up).astype(BF16)
    x2 = x1 + _dot(act, w_down_ref[...])
    o_ref[0] = _rms_scale(x2, NORM_EPS) * fin_g_ref[...]


def _out_ffn(x, nat, dft, g_na, g_df, w_na, w_df, w_o, ffn_g, w_gate, w_up, w_down, fin_g, rows):
    b, t, _ = x.shape
    per_chunk = CHUNK // rows
    const = lambda shape: pl.BlockSpec(shape, lambda i, j: (0,) * len(shape),
                                       pipeline_mode=pl.Buffered(1))
    tok_spec = pl.BlockSpec((1, rows, D_MODEL), lambda i, j: (i, j, 0))
    feat_spec = pl.BlockSpec((1, 1, 512, rows), lambda i, j: (i, j // per_chunk, 0, j % per_chunk))
    return pl.pallas_call(
        _out_ffn_kernel,
        grid=(b, t // rows),
        in_specs=[tok_spec, feat_spec, feat_spec, tok_spec, tok_spec,
                  const((NA_WIDTH, D_MODEL)), const((DIFF_WIDTH, D_MODEL)), const((D_MODEL, D_MODEL)),
                  const((1, D_MODEL)), const((D_MODEL, D_FF)), const((D_MODEL, D_FF)),
                  const((D_FF, D_MODEL)), const((1, D_MODEL))],
        out_specs=tok_spec,
        out_shape=jax.ShapeDtypeStruct(x.shape, F32),
        compiler_params=pltpu.CompilerParams(
            dimension_semantics=("parallel", "parallel"), vmem_limit_bytes=VMEM_LIMIT_BYTES),
        name="out_ffn",
    )(x, nat, dft, g_na, g_df, w_na, w_df, w_o, ffn_g, w_gate, w_up, w_down, fin_g)


def kernel(x, meta_tokens, mix_norm, w_in, na_rpb, lambda_q1, lambda_k1, lambda_q2, lambda_k2,
           diff_subln, w_na_out, w_diff_out, w_o, ffn_norm, w_gate, w_up, w_down, final_norm):
    b, t, _ = x.shape
    assert mix_norm.shape[0] == 1, "single-layer block"
    assert t % CHUNK == 0 and t // CHUNK >= 3

    w = w_in[0].astype(BF16)
    na_q, na_k, na_v, df_q, df_k, df_v, g_na, g_df = jnp.split(
        w, [512, 1024, 1536, 2048, 2560, 3072, 4096], axis=1)
    w_t = jnp.concatenate([na_q, na_v, df_q, df_v], axis=1).T
    w_s = jnp.concatenate([na_k, df_k, g_na, g_df], axis=1)
    gain = mix_norm[0][None].astype(F32)

    pos = jnp.arange(N_META + t, dtype=jnp.int32)
    tables_meta = _rope_tables(pos[:N_META])
    tables_real = _rope_tables(pos[N_META:])

    na_qt, na_vt, df_qt, df_vt, na_k_r, df_k_r, g_na_r, g_df_r = _in_proj(
        x, gain, w_t, w_s, *tables_real, rows=2 * CHUNK)
    meta = _in_proj(meta_tokens[None].astype(x.dtype), gain, w_t, w_s, *tables_meta, rows=N_META)
    na_vt_m, df_vt_m, na_k_m, df_k_m = meta[1][0], meta[3][0, 0], meta[4][0], meta[5][0]

    bias = _na_bias_tiles(na_rpb[0], t // GRID_W)
    na_out_t = _na_attention(na_qt, na_k_r, na_vt, na_k_m, na_vt_m, bias)

    lam_vecs = jnp.stack([lambda_q1[0], lambda_k1[0], lambda_q2[0], lambda_k2[0]]).astype(F32)
    df_out_t = _df_attention(df_qt, df_k_r, df_vt, df_k_m, df_vt_m, lam_vecs,
                             diff_subln[0].astype(F32)[:, None])

    return _out_ffn(x, na_out_t, df_out_t, g_na_r, g_df_r,
                    w_na_out[0].astype(BF16), w_diff_out[0].astype(BF16), w_o[0].astype(BF16),
                    ffn_norm[0][None].astype(F32), w_gate[0].astype(BF16), w_up[0].astype(BF16),
                    w_down[0].astype(BF16), final_norm[None].astype(F32), rows=512)
```

```python
import functools
import math

import jax
import jax.numpy as jnp
from jax import lax
from jax.experimental import pallas as pl
from jax.experimental.pallas import tpu as pltpu

D_MODEL = 1024
GRID_W = 64
N_META = 16
NA_HEADS = 8
NA_HEAD_DIM = 64
NA_WIN_ROWS = 8
NA_WIN_COLS = 16
DIFF_HEADS = 4
DIFF_HEAD_DIM = 64
NA_WIDTH = NA_HEADS * NA_HEAD_DIM
DIFF_WIDTH = DIFF_HEADS * 2 * DIFF_HEAD_DIM
D_FF = 2816
ROPE_THETA = 10000.0
NORM_EPS = 1e-6
SUBLN_EPS = 1e-5
LAMBDA_INIT = 0.8 - 0.6 * math.exp(-0.3 * 0)

CHUNK = 512
NA_GROUP_ROWS = CHUNK // GRID_W
NA_KEY_ROWS = 16
NA_PAIR = 2 * GRID_W
NA_PAIR_KEY_ROWS = NA_WIN_ROWS + 2
NA_QUAD = 4
NA_QUAD_DIM = NA_QUAD * NA_HEAD_DIM
V_ROWS = 128 + 16
LOG2_E = math.log2(math.e)
MASK_VALUE = -1e30
VMEM_LIMIT_BYTES = 56 * 1024 * 1024

BF16 = jnp.bfloat16
F32 = jnp.float32


def _dot(a, b):
    return jnp.dot(a, b, preferred_element_type=F32)


def _dot_nt(a, b):
    return lax.dot_general(a, b, (((1,), (1,)), ((), ())), preferred_element_type=F32)


def _dot_tn(a, b):
    return lax.dot_general(a, b, (((0,), (0,)), ((), ())), preferred_element_type=F32)


def _rms_scale(x, eps):
    return x * lax.rsqrt(jnp.mean(x * x, axis=-1, keepdims=True) + eps)


def _in_proj_kernel(x_ref, g_ref, wt_ref, ws_ref, cos_t_ref, sin_t_ref, cos_s_ref, sin_s_ref,
                    na_qt_ref, na_vt_ref, df_qt_ref, df_vt_ref, na_k_ref, df_k_ref,
                    g_na_ref, g_df_ref):
    h = (_rms_scale(x_ref[0], NORM_EPS) * g_ref[...]).astype(BF16)

    n_ch, chunk = na_qt_ref.shape[1], na_qt_ref.shape[-1]
    lanes = [slice(ci * chunk, (ci + 1) * chunk) for ci in range(n_ch)]
    nq = (_dot_nt(wt_ref[0:512], h) * (LOG2_E * NA_HEAD_DIM ** -0.5)).astype(BF16)
    for ci in range(n_ch):
        na_qt_ref[0, ci] = nq[:, lanes[ci]]
    nv = _dot_nt(wt_ref[512:1024], h).astype(BF16)
    dv = _dot_nt(wt_ref[1536:2048], h).astype(BF16)
    pad_row = lax.broadcasted_iota(jnp.int32, (V_ROWS - 128, chunk), 0)
    ones_pad = jnp.where(pad_row == 0, 1.0, 0.0).astype(BF16)
    for ci in range(n_ch):
        for hd in range(DIFF_HEADS):
            df_vt_ref[0, ci, hd, 0:128, :] = dv[hd * 128:(hd + 1) * 128, lanes[ci]]
            df_vt_ref[0, ci, hd, 128:V_ROWS, :] = ones_pad
    pair = na_vt_ref.shape[-1]
    pad_row = lax.broadcasted_iota(jnp.int32, (V_ROWS - 128, pair), 0)
    ones_pair = jnp.where(pad_row == 0, 1.0, 0.0).astype(BF16)
    for hp in range(NA_HEADS // 2):
        for pr in range(nv.shape[1] // pair):
            na_vt_ref[0, hp, pr, 0:128, :] = nv[hp * 128:(hp + 1) * 128, pr * pair:(pr + 1) * pair]
            na_vt_ref[0, hp, pr, 128:V_ROWS, :] = ones_pair
    dq = _dot_nt(wt_ref[1024:1536], h) * (LOG2_E * DIFF_HEAD_DIM ** -0.5)
    cos_t = cos_t_ref[...]
    sin_t = sin_t_ref[...]
    half = DIFF_HEAD_DIM // 2
    for c in range(DIFF_WIDTH // DIFF_HEAD_DIM):
        x1 = dq[c * 64:c * 64 + half]
        x2 = dq[c * 64 + half:(c + 1) * 64]
        top = (x1 * cos_t - x2 * sin_t).astype(BF16)
        bottom = (x2 * cos_t + x1 * sin_t).astype(BF16)
        for ci in range(n_ch):
            df_qt_ref[0, ci, c * 64:c * 64 + half, :] = top[:, lanes[ci]]
            df_qt_ref[0, ci, c * 64 + half:(c + 1) * 64, :] = bottom[:, lanes[ci]]

    na_k_ref[0] = _dot(h, ws_ref[:, 0:512]).astype(BF16)
    dk = _dot(h, ws_ref[:, 512:1024])
    cos_s = cos_s_ref[...]
    sin_s = sin_s_ref[...]
    lane = lax.broadcasted_iota(jnp.int32, cos_s.shape, 1)
    low_half = (lane % DIFF_HEAD_DIM) < half
    for c in range(DIFF_WIDTH // 128):
        xk = dk[:, c * 128:(c + 1) * 128]
        partner = jnp.where(low_half, pltpu.roll(xk, 128 - half, 1), pltpu.roll(xk, half, 1))
        df_k_ref[0, :, c * 128:(c + 1) * 128] = (xk * cos_s + partner * sin_s).astype(BF16)
    g_na_ref[0] = jax.nn.sigmoid(_dot(h, ws_ref[:, 1024:2048])).astype(BF16)
    g_df_ref[0] = jax.nn.sigmoid(_dot(h, ws_ref[:, 2048:3072])).astype(BF16)


def _in_proj(x, gain, w_t, w_s, cos_t, sin_t, cos_s, sin_s, rows):
    b, t, _ = x.shape
    n = t // rows
    const = lambda shape: pl.BlockSpec(shape, lambda i, j: (0,) * len(shape),
                                       pipeline_mode=pl.Buffered(1))
    chunk = min(CHUNK, rows)
    n_ch = rows // chunk
    feat_major = jax.ShapeDtypeStruct((b, t // chunk, 512, chunk), BF16)
    feat_spec = pl.BlockSpec((1, n_ch, 512, chunk), lambda i, j: (i, j, 0, 0))
    pair = min(NA_PAIR, rows)
    value_major = jax.ShapeDtypeStruct((b, t // chunk, 4, V_ROWS, chunk), BF16)
    value_spec = pl.BlockSpec((1, n_ch, 4, V_ROWS, chunk), lambda i, j: (i, j, 0, 0, 0))
    tok = lambda width: jax.ShapeDtypeStruct((b, t, width), BF16)
    tok_spec = lambda width: pl.BlockSpec((1, rows, width), lambda i, j: (i, j, 0))
    return pl.pallas_call(
        _in_proj_kernel,
        grid=(b, n),
        in_specs=[
            pl.BlockSpec((1, rows, D_MODEL), lambda i, j: (i, j, 0)),
            const((1, D_MODEL)),
            const((2048, D_MODEL)),
            const((D_MODEL, 3072)),
            pl.BlockSpec((DIFF_HEAD_DIM // 2, rows), lambda i, j: (0, j)),
            pl.BlockSpec((DIFF_HEAD_DIM // 2, rows), lambda i, j: (0, j)),
            pl.BlockSpec((rows, 128), lambda i, j: (j, 0)),
            pl.BlockSpec((rows, 128), lambda i, j: (j, 0)),
        ],
        out_specs=[feat_spec,
                   pl.BlockSpec((1, NA_HEADS // 2, rows // pair, V_ROWS, pair),
                                lambda i, j: (i, 0, j, 0, 0)),
                   feat_spec, value_spec,
                   tok_spec(512), tok_spec(512), tok_spec(D_MODEL), tok_spec(D_MODEL)],
        out_shape=[feat_major,
                   jax.ShapeDtypeStruct((b, NA_HEADS // 2, t // pair, V_ROWS, pair), BF16),
                   feat_major, value_major,
                   tok(512), tok(512), tok(D_MODEL), tok(D_MODEL)],
        compiler_params=pltpu.CompilerParams(
            dimension_semantics=("parallel", "parallel"), vmem_limit_bytes=VMEM_LIMIT_BYTES),
        name="in_proj",
    )(x, gain, w_t, w_s, cos_t, sin_t, cos_s, sin_s)


def _rope_tables(pos):
    half = DIFF_HEAD_DIM // 2
    inv = ROPE_THETA ** (-jnp.arange(half, dtype=F32) / half)
    ang = pos.astype(F32)[:, None] * inv[None, :]
    cos, sin = jnp.cos(ang), jnp.sin(ang)
    cos_s = jnp.tile(cos, (1, 128 // half))
    sin_s = jnp.tile(jnp.concatenate([-sin, sin], axis=1), (1, 128 // DIFF_HEAD_DIM))
    return cos.T, sin.T, cos_s, sin_s


def _na_pair_base(first_row, n_rows):
    return jnp.clip(first_row - NA_WIN_ROWS // 2, 0, n_rows - NA_PAIR_KEY_ROWS)


def _na_pair_variants(n_rows):
    assert n_rows >= 2 * NA_PAIR_KEY_ROWS + 2
    kinds = []
    for first_row in (2 * NA_WIN_ROWS, 0, 2, n_rows - 4, n_rows - 2):
        base = min(max(first_row - NA_WIN_ROWS // 2, 0), n_rows - NA_PAIR_KEY_ROWS)
        rows = []
        for r in (first_row, first_row + 1):
            start = min(max(r - NA_WIN_ROWS // 2, 0), n_rows - NA_WIN_ROWS)
            rows.append((start - base, start - base + NA_WIN_ROWS, base - r))
        kinds.append(rows)
    return kinds


def _na_pair_kind(pair_index, n_pairs):
    return jnp.where(pair_index == 0, 1, jnp.where(pair_index == 1, 2, jnp.where(
        pair_index == n_pairs - 2, 3, jnp.where(pair_index == n_pairs - 1, 4, 0))))


def _na_bias_tiles(rpb, n_rows):
    n_heads, n_dr, n_dc = rpb.shape
    kinds = _na_pair_variants(n_rows)
    mid = NA_WIN_COLS - 1
    ring = jnp.concatenate([rpb[:, :, mid::-1].astype(F32),
                            jnp.zeros((n_heads, n_dr, 128 - n_dc), F32),
                            rpb[:, :, :mid:-1].astype(F32)], axis=-1)

    def body(ring_ref, o_ref):
        key_col = lax.broadcasted_iota(jnp.int32, (GRID_W, 128), 0)
        lane = lax.broadcasted_iota(jnp.int32, (GRID_W, 128), 1)
        col_start = jnp.clip(lane % GRID_W - NA_WIN_COLS // 2, 0, GRID_W - NA_WIN_COLS)
        col_ok = (key_col >= col_start) & (key_col < col_start + NA_WIN_COLS)
        masked = jnp.full((GRID_W, 128), MASK_VALUE, F32)
        tables = []
        for d in range(n_dr):
            x = jnp.broadcast_to(ring_ref[0, d:d + 1, :], (GRID_W, 128)) * LOG2_E
            for bit in range(GRID_W.bit_length() - 1):
                x = jnp.where((key_col >> bit) & 1 == 1, pltpu.roll(x, 1 << bit, 1), x)
            x = jnp.where(lane < GRID_W, x, pltpu.roll(x, GRID_W, 1))
            tables.append(jnp.where(col_ok, x, masked))
        for v, kind in enumerate(kinds):
            for i in range(NA_PAIR_KEY_ROWS):
                halves = [tables[shift + i + NA_WIN_ROWS - 1] if lo <= i < hi else masked
                          for lo, hi, shift in kind]
                o_ref[v, 0, i * GRID_W:(i + 1) * GRID_W, :] = jnp.where(
                    lane < GRID_W, halves[0], halves[1])

    n_keys = NA_PAIR_KEY_ROWS * GRID_W
    return pl.pallas_call(
        body,
        grid=(n_heads,),
        in_specs=[pl.BlockSpec((1, n_dr, 128), lambda h: (h, 0, 0))],
        out_specs=pl.BlockSpec((len(kinds), 1, n_keys, NA_PAIR),
                               lambda h: (0, h // NA_QUAD, 0, h % NA_QUAD)),
        out_shape=jax.ShapeDtypeStruct((len(kinds), n_heads // NA_QUAD, n_keys, NA_QUAD * NA_PAIR),
                                       F32),
        compiler_params=pltpu.CompilerParams(
            dimension_semantics=("parallel",), vmem_limit_bytes=VMEM_LIMIT_BYTES),
        name="na_bias",
    )(ring)


def _na_window_row(g, n_rows):
    quarter = NA_KEY_ROWS // 4
    return jnp.clip(2 * g - 1, 0, (n_rows - NA_KEY_ROWS) // quarter) * quarter


def _na_kernel(qt_ref, k_ref, vt_ref, km_ref, vm_ref, bias_ref, o_ref, *s_refs, n_rows):
    g = pl.program_id(1)
    n_pairs = n_rows // 2
    window_row = _na_window_row(g, n_rows)
    n_keys = NA_PAIR_KEY_ROWS * GRID_W
    head_of_row = lax.broadcasted_iota(jnp.int32, (NA_QUAD_DIM, NA_PAIR), 0) // NA_HEAD_DIM
    n_quads = NA_HEADS // NA_QUAD
    units = [(jp, quad) for jp in range(NA_GROUP_ROWS // 2) for quad in range(n_quads)]

    def local_row(jp):
        return _na_pair_base(2 * (g * (NA_GROUP_ROWS // 2) + jp), n_rows) - window_row

    def scores(u):
        jp, quad = units[u]
        kind = _na_pair_kind(g * (NA_GROUP_ROWS // 2) + jp, n_pairs)
        key_off = pl.multiple_of(local_row(jp) * GRID_W, NA_PAIR)
        feat = slice(quad * NA_QUAD_DIM, (quad + 1) * NA_QUAD_DIM)
        q_pair = qt_ref[0, 0, feat, jp * NA_PAIR:(jp + 1) * NA_PAIR]
        zero = jnp.zeros_like(q_pair)
        w = jnp.concatenate([jnp.where(head_of_row == hd, q_pair, zero) for hd in range(NA_QUAD)],
                            axis=1)
        s = _dot(k_ref[0, pl.ds(key_off, n_keys), feat], w) + bias_ref[kind, quad]
        s_refs[u][...] = s
        s_meta = _dot(km_ref[:, feat], w)
        m = jnp.maximum(jnp.max(_col_part(s, jnp.max), axis=0, keepdims=True),
                        jnp.max(s_meta, axis=0, keepdims=True))
        return s_meta, m

    def values(u, s_meta, m):
        jp, quad = units[u]
        slab_off = local_row(jp) // 2
        p = jnp.exp2((s_refs[u][...] - m).astype(BF16))
        for hp in range(NA_QUAD // 2):
            slab = (NA_QUAD // 2) * quad + hp
            vt = jnp.concatenate([vt_ref[0, slab, slab_off + i] for i in range(NA_PAIR_KEY_ROWS // 2)],
                                 axis=1)
            cols = slice(2 * hp * NA_PAIR, 2 * (hp + 1) * NA_PAIR)
            p_meta = jnp.exp2((s_meta[:, cols] - m[:, cols]).astype(BF16))
            o = _dot(vt, p[:, cols]) + _dot(vm_ref[slab, 0], p_meta)
            denom = o[128:129]
            for hd in range(2):
                first = quad * NA_QUAD_DIM + (2 * hp + hd) * NA_HEAD_DIM
                o_ref[0, 0, first:first + NA_HEAD_DIM, jp * NA_PAIR:(jp + 1) * NA_PAIR] = (
                    o[hd * NA_HEAD_DIM:(hd + 1) * NA_HEAD_DIM, hd * NA_PAIR:(hd + 1) * NA_PAIR]
                    / denom[:, hd * NA_PAIR:(hd + 1) * NA_PAIR]).astype(BF16)

    pending = scores(0)
    for u in range(len(units)):
        following = scores(u + 1) if u + 1 < len(units) else None
        values(u, *pending)
        pending = following


def _na_attention(qt, k, vt, k_meta, vt_meta, bias):
    b, n_groups, _, _ = qt.shape
    n_rows = n_groups * NA_GROUP_ROWS
    n_quads = NA_HEADS // NA_QUAD
    n_kinds, _, n_keys, _ = bias.shape
    return pl.pallas_call(
        functools.partial(_na_kernel, n_rows=n_rows),
        grid=(b, n_groups),
        in_specs=[
            pl.BlockSpec((1, 1, NA_WIDTH, CHUNK), lambda bi, g: (bi, g, 0, 0)),
            pl.BlockSpec((pl.Element(1), pl.Element(NA_KEY_ROWS * GRID_W), pl.Element(NA_WIDTH)),
                         lambda bi, g: (bi, _na_window_row(g, n_rows) * GRID_W, 0)),
            pl.BlockSpec((pl.Element(1), pl.Element(NA_HEADS // 2), pl.Element(NA_KEY_ROWS // 2),
                          pl.Element(V_ROWS), pl.Element(NA_PAIR)),
                         lambda bi, g: (bi, 0, _na_window_row(g, n_rows) // 2, 0, 0)),
            pl.BlockSpec((N_META, NA_WIDTH), lambda bi, g: (0, 0)),
            pl.BlockSpec((NA_HEADS // 2, 1, V_ROWS, N_META), lambda bi, g: (0, 0, 0, 0)),
            pl.BlockSpec((n_kinds, n_quads, n_keys, NA_QUAD * NA_PAIR), lambda bi, g: (0, 0, 0, 0),
                         pipeline_mode=pl.Buffered(1)),
        ],
        out_specs=pl.BlockSpec((1, 1, NA_WIDTH, CHUNK), lambda bi, g: (bi, g, 0, 0)),
        out_shape=jax.ShapeDtypeStruct(qt.shape, BF16),
        scratch_shapes=[pltpu.VMEM((n_keys, NA_QUAD * NA_PAIR), F32)]
        * (n_quads * NA_GROUP_ROWS // 2),
        compiler_params=pltpu.CompilerParams(
            dimension_semantics=("parallel", "parallel"), vmem_limit_bytes=VMEM_LIMIT_BYTES),
        name="na_attn",
    )(qt, k, vt, k_meta, vt_meta, bias)


def _col_part(x, op):
    return op(x.reshape(x.shape[0] // 8, 8, x.shape[1]), axis=0)


def _df_kernel(qt_ref, k_ref, vt_ref, km_ref, vm_ref, lam_ref, g_ref, o_ref,
               s_a0, s_a1, s_b0, s_b1, p_a0, p_a1, p_b0, p_b1, acc0_ref, acc1_ref, *, n_chunks):
    s_even, s_odd = (s_a0, s_a1), (s_b0, s_b1)
    p_even, p_odd = (p_a0, p_a1), (p_b0, p_b1)
    acc_refs = (acc0_ref, acc1_ref)
    n_q = n_chunks
    k_meta = km_ref[...]
    vt_meta = vm_ref[0]
    lam_v = lam_ref[...]
    lam = (jnp.exp(jnp.sum(lam_v[0:1] * lam_v[1:2], axis=1, keepdims=True))
           - jnp.exp(jnp.sum(lam_v[2:3] * lam_v[3:4], axis=1, keepdims=True)) + LAMBDA_INIT)

    def masked_q(qb):
        qt = qt_ref[0, qb]
        row = lax.broadcasted_iota(jnp.int32, qt.shape, 0)
        zero = jnp.zeros_like(qt)
        return (jnp.where(row < DIFF_HEAD_DIM, qt, zero), jnp.where(row >= DIFF_HEAD_DIM, qt, zero))

    def qk(qts, chunk, s_bufs):
        kc = k_ref[0, chunk]
        part = []
        for c in range(2):
            s = _dot(kc, qts[c])
            s_bufs[c][...] = s
            part.append(_col_part(s, jnp.max))
        return part

    def new_max(m, part):
        m_new = [jnp.maximum(m[c], jnp.max(part[c], axis=0, keepdims=True)) for c in range(2)]
        return m_new, [jnp.exp2(m[c] - m_new[c]) for c in range(2)]

    def expo(s_bufs, p_bufs, m):
        for c in range(2):
            p_bufs[c][...] = jnp.exp2((s_bufs[c][...] - m[c]).astype(BF16))

    def pv(p_bufs, chunk, alpha):
        vc = vt_ref[0, chunk, 0]
        for c in range(2):
            acc_refs[c][...] = acc_refs[c][...] * alpha[c] + _dot(vc, p_bufs[c][...])

    def finish(qb):
        o = (acc0_ref[0:128] / acc0_ref[128:129]
             - lam * (acc1_ref[0:128] / acc1_ref[128:129]))
        o = o * lax.rsqrt(jnp.mean(o * o, axis=0, keepdims=True) + SUBLN_EPS)
        o_ref[0, qb] = (o * g_ref[...] * (1.0 - LAMBDA_INIT)).astype(BF16)

    def handover(qb, cur):
        nxt = jnp.minimum(qb + 1, n_q - 1)
        qts = masked_q(nxt)
        part0 = qk(qts, 0, s_even)
        s_meta = [_dot(k_meta, qts[c]) for c in range(2)]
        if cur is not None:
            expo(s_odd, p_odd, cur[0])
            pv(p_even, n_chunks - 2, cur[2])
        m0 = [jnp.maximum(jnp.max(part0[c], axis=0, keepdims=True),
                          jnp.max(s_meta[c], axis=0, keepdims=True)) for c in range(2)]
        part1 = qk(qts, 1, s_odd)
        expo(s_even, p_even, m0)
        if cur is not None:
            pv(p_odd, n_chunks - 1, cur[1])
            finish(qb)
        for c in range(2):
            acc_refs[c][...] = _dot(vt_meta, jnp.exp2((s_meta[c] - m0[c]).astype(BF16)))
        m1, a1 = new_max(m0, part1)
        return tuple(m1), tuple(a1), tuple(jnp.ones_like(x) for x in m1)

    def steps(qts, t, m_e, a_e, a_p):
        part = qk(qts, t + 2, s_even)
        expo(s_odd, p_odd, m_e)
        pv(p_even, t, a_p)
        m_2, a_2 = new_max(m_e, part)
        part = qk(qts, t + 3, s_odd)
        expo(s_even, p_even, m_2)
        pv(p_odd, t + 1, a_e)
        m_3, a_3 = new_max(m_2, part)
        return tuple(m_3), tuple(a_3), tuple(a_2)

    def block(qb, carry):
        qts = masked_q(qb)
        for i in range(n_chunks // 2 - 1):
            carry = steps(qts, 2 * i, *carry)
        return handover(qb, carry)

    assert n_chunks % 2 == 0 and n_chunks >= 4
    lax.fori_loop(0, n_q, block, handover(-1, None))


def _df_attention(qt, k, vt, k_meta, vt_meta, lam_vecs, subln_col):
    b, n_chunks, _, _ = qt.shape
    k4 = k.reshape(b, n_chunks, CHUNK, DIFF_WIDTH)
    return pl.pallas_call(
        functools.partial(_df_kernel, n_chunks=n_chunks),
        grid=(b, DIFF_HEADS),
        in_specs=[
            pl.BlockSpec((1, n_chunks, 128, CHUNK), lambda bi, h: (bi, 0, h, 0)),
            pl.BlockSpec((1, n_chunks, CHUNK, 128), lambda bi, h: (bi, 0, 0, h)),
            pl.BlockSpec((1, n_chunks, 1, V_ROWS, CHUNK), lambda bi, h: (bi, 0, h, 0, 0)),
            pl.BlockSpec((N_META, 128), lambda bi, h: (0, h)),
            pl.BlockSpec((1, V_ROWS, N_META), lambda bi, h: (h, 0, 0)),
            pl.BlockSpec((4, DIFF_HEAD_DIM), lambda bi, h: (0, 0)),
            pl.BlockSpec((128, 1), lambda bi, h: (0, 0)),
        ],
        out_specs=pl.BlockSpec((1, n_chunks, 128, CHUNK), lambda bi, h: (bi, 0, h, 0)),
        out_shape=jax.ShapeDtypeStruct(qt.shape, BF16),
        scratch_shapes=[pltpu.VMEM((CHUNK, CHUNK), F32)] * 4
        + [pltpu.VMEM((CHUNK, CHUNK), BF16)] * 4
        + [pltpu.VMEM((V_ROWS, CHUNK), F32)] * 2,
        compiler_params=pltpu.CompilerParams(
            dimension_semantics=("parallel", "parallel"), vmem_limit_bytes=VMEM_LIMIT_BYTES),
        name="df_attn",
    )(qt, k4, vt, k_meta, vt_meta, lam_vecs, subln_col)


def _out_ffn_kernel(x_ref, nat_ref, dft_ref, g_na_ref, g_df_ref, w_na_ref, w_df_ref, w_o_ref,
                    ffn_g_ref, w_gate_ref, w_up_ref, w_down_ref, fin_g_ref, o_ref):
    o_na = _dot_tn(nat_ref[0, 0], w_na_ref[...])
    o_df = _dot_tn(dft_ref[0, 0], w_df_ref[...])
    merged = g_na_ref[0].astype(F32) * o_na + g_df_ref[0].astype(F32) * o_df
    x1 = x_ref[0] + _dot(merged.astype(BF16), w_o_ref[...])
    h = (_rms_scale(x1, NORM_EPS) * ffn_g_ref[...]).astype(BF16)
    gate = _dot(h, w_gate_ref[...])
    up = _dot(h, w_up_ref[...])
    act = (gate * jax.nn.sigmoid(gate) * up).astype(BF16)
    x2 = x1 + _dot(act, w_down_ref[...])
    o_ref[0] = _rms_scale(x2, NORM_EPS) * fin_g_ref[...]


def _out_ffn(x, nat, dft, g_na, g_df, w_na, w_df, w_o, ffn_g, w_gate, w_up, w_down, fin_g, rows):
    b, t, _ = x.shape
    per_chunk = CHUNK // rows
    const = lambda shape: pl.BlockSpec(shape, lambda i, j: (0,) * len(shape),
                                       pipeline_mode=pl.Buffered(1))
    tok_spec = pl.BlockSpec((1, rows, D_MODEL), lambda i, j: (i, j, 0))
    feat_spec = pl.BlockSpec((1, 1, 512, rows), lambda i, j: (i, j // per_chunk, 0, j % per_chunk))
    return pl.pallas_call(
        _out_ffn_kernel,
        grid=(b, t // rows),
        in_specs=[tok_spec, feat_spec, feat_spec, tok_spec, tok_spec,
                  const((NA_WIDTH, D_MODEL)), const((DIFF_WIDTH, D_MODEL)), const((D_MODEL, D_MODEL)),
                  const((1, D_MODEL)), const((D_MODEL, D_FF)), const((D_MODEL, D_FF)),
                  const((D_FF, D_MODEL)), const((1, D_MODEL))],
        out_specs=tok_spec,
        out_shape=jax.ShapeDtypeStruct(x.shape, F32),
        compiler_params=pltpu.CompilerParams(
            dimension_semantics=("parallel", "parallel"), vmem_limit_bytes=VMEM_LIMIT_BYTES),
        name="out_ffn",
    )(x, nat, dft, g_na, g_df, w_na, w_df, w_o, ffn_g, w_gate, w_up, w_down, fin_g)


def kernel(x, meta_tokens, mix_norm, w_in, na_rpb, lambda_q1, lambda_k1, lambda_q2, lambda_k2,
           diff_subln, w_na_out, w_diff_out, w_o, ffn_norm, w_gate, w_up, w_down, final_norm):
    b, t, _ = x.shape
    assert mix_norm.shape[0] == 1, "single-layer block"
    assert t % CHUNK == 0 and t // CHUNK >= 3

    w = w_in[0].astype(BF16)
    na_q, na_k, na_v, df_q, df_k, df_v, g_na, g_df = jnp.split(
        w, [512, 1024, 1536, 2048, 2560, 3072, 4096], axis=1)
    w_t = jnp.concatenate([na_q, na_v, df_q, df_v], axis=1).T
    w_s = jnp.concatenate([na_k, df_k, g_na, g_df], axis=1)
    gain = mix_norm[0][None].astype(F32)

    pos = jnp.arange(N_META + t, dtype=jnp.int32)
    tables_meta = _rope_tables(pos[:N_META])
    tables_real = _rope_tables(pos[N_META:])

    na_qt, na_vt, df_qt, df_vt, na_k_r, df_k_r, g_na_r, g_df_r = _in_proj(
        x, gain, w_t, w_s, *tables_real, rows=2 * CHUNK)
    meta = _in_proj(meta_tokens[None].astype(x.dtype), gain, w_t, w_s, *tables_meta, rows=N_META)
    na_vt_m, df_vt_m, na_k_m, df_k_m = meta[1][0], meta[3][0, 0], meta[4][0], meta[5][0]

    bias = _na_bias_tiles(na_rpb[0], t // GRID_W)
    na_out_t = _na_attention(na_qt, na_k_r, na_vt, na_k_m, na_vt_m, bias)

    lam_vecs = jnp.stack([lambda_q1[0], lambda_k1[0], lambda_q2[0], lambda_k2[0]]).astype(F32)
    df_out_t = _df_attention(df_qt, df_k_r, df_vt, df_k_m, df_vt_m, lam_vecs,
                             diff_subln[0].astype(F32)[:, None])

    return _out_ffn(x, na_out_t, df_out_t, g_na_r, g_df_r,
                    w_na_out[0].astype(BF16), w_diff_out[0].astype(BF16), w_o[0].astype(BF16),
                    ffn_norm[0][None].astype(F32), w_gate[0].astype(BF16), w_up[0].astype(BF16),
                    w_down[0].astype(BF16), final_norm[None].astype(F32), rows=512)
```

```python
import functools
import math

import jax
import jax.numpy as jnp
from jax import lax
from jax.experimental import pallas as pl
from jax.experimental.pallas import tpu as pltpu

D_MODEL = 1024
GRID_W = 64
N_META = 16
NA_HEADS = 8
NA_HEAD_DIM = 64
NA_WIN_ROWS = 8
NA_WIN_COLS = 16
DIFF_HEADS = 4
DIFF_HEAD_DIM = 64
NA_WIDTH = NA_HEADS * NA_HEAD_DIM
DIFF_WIDTH = DIFF_HEADS * 2 * DIFF_HEAD_DIM
D_FF = 2816
ROPE_THETA = 10000.0
NORM_EPS = 1e-6
SUBLN_EPS = 1e-5
LAMBDA_INIT = 0.8 - 0.6 * math.exp(-0.3 * 0)

CHUNK = 512
NA_GROUP_ROWS = CHUNK // GRID_W
NA_KEY_ROWS = 16
NA_PAIR = 2 * GRID_W
NA_PAIR_KEY_ROWS = NA_WIN_ROWS + 2
NA_QUAD = 4
NA_QUAD_DIM = NA_QUAD * NA_HEAD_DIM
V_ROWS = 128 + 16
LOG2_E = math.log2(math.e)
MASK_VALUE = -1e30
VMEM_LIMIT_BYTES = 56 * 1024 * 1024

BF16 = jnp.bfloat16
F32 = jnp.float32


def _dot(a, b):
    return jnp.dot(a, b, preferred_element_type=F32)


def _dot_nt(a, b):
    return lax.dot_general(a, b, (((1,), (1,)), ((), ())), preferred_element_type=F32)


def _dot_tn(a, b):
    return lax.dot_general(a, b, (((0,), (0,)), ((), ())), preferred_element_type=F32)


def _rms_scale(x, eps):
    return x * lax.rsqrt(jnp.mean(x * x, axis=-1, keepdims=True) + eps)


def _in_proj_kernel(x_ref, g_ref, wt_ref, ws_ref, cos_t_ref, sin_t_ref, cos_s_ref, sin_s_ref,
                    na_qt_ref, na_vt_ref, df_qt_ref, df_vt_ref, na_k_ref, df_k_ref,
                    g_na_ref, g_df_ref):
    h = (_rms_scale(x_ref[0], NORM_EPS) * g_ref[...]).astype(BF16)

    n_ch, chunk = na_qt_ref.shape[1], na_qt_ref.shape[-1]
    lanes = [slice(ci * chunk, (ci + 1) * chunk) for ci in range(n_ch)]
    nq = (_dot_nt(wt_ref[0:512], h) * (LOG2_E * NA_HEAD_DIM ** -0.5)).astype(BF16)
    for ci in range(n_ch):
        na_qt_ref[0, ci] = nq[:, lanes[ci]]
    nv = _dot_nt(wt_ref[512:1024], h).astype(BF16)
    dv = _dot_nt(wt_ref[1536:2048], h).astype(BF16)
    pad_row = lax.broadcasted_iota(jnp.int32, (V_ROWS - 128, chunk), 0)
    ones_pad = jnp.where(pad_row == 0, 1.0, 0.0).astype(BF16)
    for ci in range(n_ch):
        for hd in range(DIFF_HEADS):
            df_vt_ref[0, ci, hd, 0:128, :] = dv[hd * 128:(hd + 1) * 128, lanes[ci]]
            df_vt_ref[0, ci, hd, 128:V_ROWS, :] = ones_pad
    pair = na_vt_ref.shape[-1]
    pad_row = lax.broadcasted_iota(jnp.int32, (V_ROWS - 128, pair), 0)
    ones_pair = jnp.where(pad_row == 0, 1.0, 0.0).astype(BF16)
    for hp in range(NA_HEADS // 2):
        for pr in range(nv.shape[1] // pair):
            na_vt_ref[0, hp, pr, 0:128, :] = nv[hp * 128:(hp + 1) * 128, pr * pair:(pr + 1) * pair]
            na_vt_ref[0, hp, pr, 128:V_ROWS, :] = ones_pair
    dq = _dot_nt(wt_ref[1024:1536], h) * (LOG2_E * DIFF_HEAD_DIM ** -0.5)
    cos_t = cos_t_ref[...]
    sin_t = sin_t_ref[...]
    half = DIFF_HEAD_DIM // 2
    for c in range(DIFF_WIDTH // DIFF_HEAD_DIM):
        x1 = dq[c * 64:c * 64 + half]
        x2 = dq[c * 64 + half:(c + 1) * 64]
        top = (x1 * cos_t - x2 * sin_t).astype(BF16)
        bottom = (x2 * cos_t + x1 * sin_t).astype(BF16)
        for ci in range(n_ch):
            df_qt_ref[0, ci, c * 64:c * 64 + half, :] = top[:, lanes[ci]]
            df_qt_ref[0, ci, c * 64 + half:(c + 1) * 64, :] = bottom[:, lanes[ci]]

    na_k_ref[0] = _dot(h, ws_ref[:, 0:512]).astype(BF16)
    dk = _dot(h, ws_ref[:, 512:1024])
    cos_s = cos_s_ref[...]
    sin_s = sin_s_ref[...]
    lane = lax.broadcasted_iota(jnp.int32, cos_s.shape, 1)
    low_half = (lane % DIFF_HEAD_DIM) < half
    for c in range(DIFF_WIDTH // 128):
        xk = dk[:, c * 128:(c + 1) * 128]
        partner = jnp.where(low_half, pltpu.roll(xk, 128 - half, 1), pltpu.roll(xk, half, 1))
        df_k_ref[0, :, c * 128:(c + 1) * 128] = (xk * cos_s + partner * sin_s).astype(BF16)
    g_na_ref[0] = jax.nn.sigmoid(_dot(h, ws_ref[:, 1024:2048])).astype(BF16)
    g_df_ref[0] = jax.nn.sigmoid(_dot(h, ws_ref[:, 2048:3072])).astype(BF16)


def _in_proj(x, gain, w_t, w_s, cos_t, sin_t, cos_s, sin_s, rows):
    b, t, _ = x.shape
    n = t // rows
    const = lambda shape: pl.BlockSpec(shape, lambda i, j: (0,) * len(shape),
                                       pipeline_mode=pl.Buffered(1))
    chunk = min(CHUNK, rows)
    n_ch = rows // chunk
    feat_major = jax.ShapeDtypeStruct((b, t // chunk, 512, chunk), BF16)
    feat_spec = pl.BlockSpec((1, n_ch, 512, chunk), lambda i, j: (i, j, 0, 0))
    pair = min(NA_PAIR, rows)
    value_major = jax.ShapeDtypeStruct((b, t // chunk, 4, V_ROWS, chunk), BF16)
    value_spec = pl.BlockSpec((1, n_ch, 4, V_ROWS, chunk), lambda i, j: (i, j, 0, 0, 0))
    tok = lambda width: jax.ShapeDtypeStruct((b, t, width), BF16)
    tok_spec = lambda width: pl.BlockSpec((1, rows, width), lambda i, j: (i, j, 0))
    return pl.pallas_call(
        _in_proj_kernel,
        grid=(b, n),
        in_specs=[
            pl.BlockSpec((1, rows, D_MODEL), lambda i, j: (i, j, 0)),
            const((1, D_MODEL)),
            const((2048, D_MODEL)),
            const((D_MODEL, 3072)),
            pl.BlockSpec((DIFF_HEAD_DIM // 2, rows), lambda i, j: (0, j)),
            pl.BlockSpec((DIFF_HEAD_DIM // 2, rows), lambda i, j: (0, j)),
            pl.BlockSpec((rows, 128), lambda i, j: (j, 0)),
            pl.BlockSpec((rows, 128), lambda i, j: (j, 0)),
        ],
        out_specs=[feat_spec,
                   pl.BlockSpec((1, NA_HEADS // 2, rows // pair, V_ROWS, pair),
                                lambda i, j: (i, 0, j, 0, 0)),
                   feat_spec, value_spec,
                   tok_spec(512), tok_spec(512), tok_spec(D_MODEL), tok_spec(D_MODEL)],
        out_shape=[feat_major,
                   jax.ShapeDtypeStruct((b, NA_HEADS // 2, t // pair, V_ROWS, pair), BF16),
                   feat_major, value_major,
                   tok(512), tok(512), tok(D_MODEL), tok(D_MODEL)],
        compiler_params=pltpu.CompilerParams(
            dimension_semantics=("parallel", "parallel"), vmem_limit_bytes=VMEM_LIMIT_BYTES),
        name="in_proj",
    )(x, gain, w_t, w_s, cos_t, sin_t, cos_s, sin_s)


def _rope_tables(pos):
    half = DIFF_HEAD_DIM // 2
    inv = ROPE_THETA ** (-jnp.arange(half, dtype=F32) / half)
    ang = pos.astype(F32)[:, None] * inv[None, :]
    cos, sin = jnp.cos(ang), jnp.sin(ang)
    cos_s = jnp.tile(cos, (1, 128 // half))
    sin_s = jnp.tile(jnp.concatenate([-sin, sin], axis=1), (1, 128 // DIFF_HEAD_DIM))
    return cos.T, sin.T, cos_s, sin_s


def _na_pair_base(first_row, n_rows):
    return jnp.clip(first_row - NA_WIN_ROWS // 2, 0, n_rows - NA_PAIR_KEY_ROWS)


def _na_pair_variants(n_rows):
    assert n_rows >= 2 * NA_PAIR_KEY_ROWS + 2
    kinds = []
    for first_row in (2 * NA_WIN_ROWS, 0, 2, n_rows - 4, n_rows - 2):
        base = min(max(first_row - NA_WIN_ROWS // 2, 0), n_rows - NA_PAIR_KEY_ROWS)
        rows = []
        for r in (first_row, first_row + 1):
            start = min(max(r - NA_WIN_ROWS // 2, 0), n_rows - NA_WIN_ROWS)
            rows.append((start - base, start - base + NA_WIN_ROWS, base - r))
        kinds.append(rows)
    return kinds


def _na_pair_kind(pair_index, n_pairs):
    return jnp.where(pair_index == 0, 1, jnp.where(pair_index == 1, 2, jnp.where(
        pair_index == n_pairs - 2, 3, jnp.where(pair_index == n_pairs - 1, 4, 0))))


def _na_bias_tiles(rpb, n_rows):
    n_heads, n_dr, n_dc = rpb.shape
    kinds = _na_pair_variants(n_rows)
    mid = NA_WIN_COLS - 1
    ring = jnp.concatenate([rpb[:, :, mid::-1].astype(F32),
                            jnp.zeros((n_heads, n_dr, 128 - n_dc), F32),
                            rpb[:, :, :mid:-1].astype(F32)], axis=-1)

    def body(ring_ref, o_ref):
        key_col = lax.broadcasted_iota(jnp.int32, (GRID_W, 128), 0)
        lane = lax.broadcasted_iota(jnp.int32, (GRID_W, 128), 1)
        col_start = jnp.clip(lane % GRID_W - NA_WIN_COLS // 2, 0, GRID_W - NA_WIN_COLS)
        col_ok = (key_col >= col_start) & (key_col < col_start + NA_WIN_COLS)
        masked = jnp.full((GRID_W, 128), MASK_VALUE, F32)
        tables = []
        for d in range(n_dr):
            x = jnp.broadcast_to(ring_ref[0, d:d + 1, :], (GRID_W, 128)) * LOG2_E
            for bit in range(GRID_W.bit_length() - 1):
                x = jnp.where((key_col >> bit) & 1 == 1, pltpu.roll(x, 1 << bit, 1), x)
            x = jnp.where(lane < GRID_W, x, pltpu.roll(x, GRID_W, 1))
            tables.append(jnp.where(col_ok, x, masked))
        for v, kind in enumerate(kinds):
            for i in range(NA_PAIR_KEY_ROWS):
                halves = [tables[shift + i + NA_WIN_ROWS - 1] if lo <= i < hi else masked
                          for lo, hi, shift in kind]
                o_ref[v, 0, i * GRID_W:(i + 1) * GRID_W, :] = jnp.where(
                    lane < GRID_W, halves[0], halves[1])

    n_keys = NA_PAIR_KEY_ROWS * GRID_W
    return pl.pallas_call(
        body,
        grid=(n_heads,),
        in_specs=[pl.BlockSpec((1, n_dr, 128), lambda h: (h, 0, 0))],
        out_specs=pl.BlockSpec((len(kinds), 1, n_keys, NA_PAIR),
                               lambda h: (0, h // NA_QUAD, 0, h % NA_QUAD)),
        out_shape=jax.ShapeDtypeStruct((len(kinds), n_heads // NA_QUAD, n_keys, NA_QUAD * NA_PAIR),
                                       F32),
        compiler_params=pltpu.CompilerParams(
            dimension_semantics=("parallel",), vmem_limit_bytes=VMEM_LIMIT_BYTES),
        name="na_bias",
    )(ring)


def _na_window_row(g, n_rows):
    quarter = NA_KEY_ROWS // 4
    return jnp.clip(2 * g - 1, 0, (n_rows - NA_KEY_ROWS) // quarter) * quarter


def _na_kernel(qt_ref, k_ref, vt_ref, km_ref, vm_ref, bias_ref, o_ref, *s_refs, n_rows):
    g = pl.program_id(1)
    n_pairs = n_rows // 2
    window_row = _na_window_row(g, n_rows)
    n_keys = NA_PAIR_KEY_ROWS * GRID_W
    head_of_row = lax.broadcasted_iota(jnp.int32, (NA_QUAD_DIM, NA_PAIR), 0) // NA_HEAD_DIM
    n_quads = NA_HEADS // NA_QUAD
    units = [(jp, quad) for jp in range(NA_GROUP_ROWS // 2) for quad in range(n_quads)]

    def local_row(jp):
        return _na_pair_base(2 * (g * (NA_GROUP_ROWS // 2) + jp), n_rows) - window_row

    def scores(u):
        jp, quad = units[u]
        kind = _na_pair_kind(g * (NA_GROUP_ROWS // 2) + jp, n_pairs)
        key_off = pl.multiple_of(local_row(jp) * GRID_W, NA_PAIR)
        halves, meta_halves = [], []
        for hp in range(NA_QUAD // 2):
            feat = slice(quad * NA_QUAD_DIM + hp * 128, quad * NA_QUAD_DIM + (hp + 1) * 128)
            q_pair = qt_ref[0, 0, feat, jp * NA_PAIR:(jp + 1) * NA_PAIR]
            zero = jnp.zeros_like(q_pair)
            w = jnp.concatenate([jnp.where(head_of_row[0:128] == hd, q_pair, zero) for hd in range(2)],
                                axis=1)
            halves.append(_dot(k_ref[0, pl.ds(key_off, n_keys), feat], w))
            meta_halves.append(_dot(km_ref[:, feat], w))
        s = jnp.concatenate(halves, axis=1) + bias_ref[kind, quad]
        s_refs[u][...] = s
        s_meta = jnp.concatenate(meta_halves, axis=1)
        m = jnp.maximum(jnp.max(_col_part(s, jnp.max), axis=0, keepdims=True),
                        jnp.max(s_meta, axis=0, keepdims=True))
        return s_meta, m

    def values(u, s_meta, m):
        jp, quad = units[u]
        slab_off = local_row(jp) // 2
        for hp in range(NA_QUAD // 2):
            slab = (NA_QUAD // 2) * quad + hp
            vt = jnp.concatenate([vt_ref[0, slab, slab_off + i] for i in range(NA_PAIR_KEY_ROWS // 2)],
                                 axis=1)
            cols = slice(2 * hp * NA_PAIR, 2 * (hp + 1) * NA_PAIR)
            p = jnp.exp2((s_refs[u][:, cols] - m[:, cols]).astype(BF16))
            p_meta = jnp.exp2((s_meta[:, cols] - m[:, cols]).astype(BF16))
            o = _dot(vt, p) + _dot(vm_ref[slab, 0], p_meta)
            denom = o[128:129]
            for hd in range(2):
                first = quad * NA_QUAD_DIM + (2 * hp + hd) * NA_HEAD_DIM
                o_ref[0, 0, first:first + NA_HEAD_DIM, jp * NA_PAIR:(jp + 1) * NA_PAIR] = (
                    o[hd * NA_HEAD_DIM:(hd + 1) * NA_HEAD_DIM, hd * NA_PAIR:(hd + 1) * NA_PAIR]
                    / denom[:, hd * NA_PAIR:(hd + 1) * NA_PAIR]).astype(BF16)

    pending = scores(0)
    for u in range(len(units)):
        following = scores(u + 1) if u + 1 < len(units) else None
        values(u, *pending)
        pending = following


def _na_attention(qt, k, vt, k_meta, vt_meta, bias):
    b, n_groups, _, _ = qt.shape
    n_rows = n_groups * NA_GROUP_ROWS
    n_quads = NA_HEADS // NA_QUAD
    n_kinds, _, n_keys, _ = bias.shape
    return pl.pallas_call(
        functools.partial(_na_kernel, n_rows=n_rows),
        grid=(b, n_groups),
        in_specs=[
            pl.BlockSpec((1, 1, NA_WIDTH, CHUNK), lambda bi, g: (bi, g, 0, 0)),
            pl.BlockSpec((pl.Element(1), pl.Element(NA_KEY_ROWS * GRID_W), pl.Element(NA_WIDTH)),
                         lambda bi, g: (bi, _na_window_row(g, n_rows) * GRID_W, 0)),
            pl.BlockSpec((pl.Element(1), pl.Element(NA_HEADS // 2), pl.Element(NA_KEY_ROWS // 2),
                          pl.Element(V_ROWS), pl.Element(NA_PAIR)),
                         lambda bi, g: (bi, 0, _na_window_row(g, n_rows) // 2, 0, 0)),
            pl.BlockSpec((N_META, NA_WIDTH), lambda bi, g: (0, 0)),
            pl.BlockSpec((NA_HEADS // 2, 1, V_ROWS, N_META), lambda bi, g: (0, 0, 0, 0)),
            pl.BlockSpec((n_kinds, n_quads, n_keys, NA_QUAD * NA_PAIR), lambda bi, g: (0, 0, 0, 0),
                         pipeline_mode=pl.Buffered(1)),
        ],
        out_specs=pl.BlockSpec((1, 1, NA_WIDTH, CHUNK), lambda bi, g: (bi, g, 0, 0)),
        out_shape=jax.ShapeDtypeStruct(qt.shape, BF16),
        scratch_shapes=[pltpu.VMEM((n_keys, NA_QUAD * NA_PAIR), F32)]
        * (n_quads * NA_GROUP_ROWS // 2),
        compiler_params=pltpu.CompilerParams(
            dimension_semantics=("parallel", "parallel"), vmem_limit_bytes=VMEM_LIMIT_BYTES),
        name="na_attn",
    )(qt, k, vt, k_meta, vt_meta, bias)


def _col_part(x, op):
    return op(x.reshape(x.shape[0] // 8, 8, x.shape[1]), axis=0)


def _df_kernel(qt_ref, k_ref, vt_ref, km_ref, vm_ref, lam_ref, g_ref, o_ref,
               s_a0, s_a1, s_b0, s_b1, p_a0, p_a1, p_b0, p_b1, acc0_ref, acc1_ref, *, n_chunks):
    s_even, s_odd = (s_a0, s_a1), (s_b0, s_b1)
    p_even, p_odd = (p_a0, p_a1), (p_b0, p_b1)
    acc_refs = (acc0_ref, acc1_ref)
    n_q = n_chunks
    k_meta = km_ref[...]
    vt_meta = vm_ref[0]
    lam_v = lam_ref[...]
    lam = (jnp.exp(jnp.sum(lam_v[0:1] * lam_v[1:2], axis=1, keepdims=True))
           - jnp.exp(jnp.sum(lam_v[2:3] * lam_v[3:4], axis=1, keepdims=True)) + LAMBDA_INIT)

    def masked_q(qb):
        qt = qt_ref[0, qb]
        row = lax.broadcasted_iota(jnp.int32, qt.shape, 0)
        zero = jnp.zeros_like(qt)
        return (jnp.where(row < DIFF_HEAD_DIM, qt, zero), jnp.where(row >= DIFF_HEAD_DIM, qt, zero))

    def qk(qts, chunk, s_bufs):
        kc = k_ref[0, chunk]
        part = []
        for c in range(2):
            s = _dot(kc, qts[c])
            s_bufs[c][...] = s
            part.append(_col_part(s, jnp.max))
        return part

    def new_max(m, part):
        m_new = [jnp.maximum(m[c], jnp.max(part[c], axis=0, keepdims=True)) for c in range(2)]
        return m_new, [jnp.exp2(m[c] - m_new[c]) for c in range(2)]

    def expo(s_bufs, p_bufs, m):
        for c in range(2):
            p_bufs[c][...] = jnp.exp2((s_bufs[c][...] - m[c]).astype(BF16))

    def pv(p_bufs, chunk, alpha):
        vc = vt_ref[0, chunk, 0]
        for c in range(2):
            acc_refs[c][...] = acc_refs[c][...] * alpha[c] + _dot(vc, p_bufs[c][...])

    def finish(qb):
        o = (acc0_ref[0:128] / acc0_ref[128:129]
             - lam * (acc1_ref[0:128] / acc1_ref[128:129]))
        o = o * lax.rsqrt(jnp.mean(o * o, axis=0, keepdims=True) + SUBLN_EPS)
        o_ref[0, qb] = (o * g_ref[...] * (1.0 - LAMBDA_INIT)).astype(BF16)

    def handover(qb, cur):
        nxt = jnp.minimum(qb + 1, n_q - 1)
        qts = masked_q(nxt)
        part0 = qk(qts, 0, s_even)
        s_meta = [_dot(k_meta, qts[c]) for c in range(2)]
        if cur is not None:
            expo(s_odd, p_odd, cur[0])
            pv(p_even, n_chunks - 2, cur[2])
        m0 = [jnp.maximum(jnp.max(part0[c], axis=0, keepdims=True),
                          jnp.max(s_meta[c], axis=0, keepdims=True)) for c in range(2)]
        part1 = qk(qts, 1, s_odd)
        expo(s_even, p_even, m0)
        if cur is not None:
            pv(p_odd, n_chunks - 1, cur[1])
            finish(qb)
        for c in range(2):
            acc_refs[c][...] = _dot(vt_meta, jnp.exp2((s_meta[c] - m0[c]).astype(BF16)))
        m1, a1 = new_max(m0, part1)
        return tuple(m1), tuple(a1), tuple(jnp.ones_like(x) for x in m1)

    def steps(qts, t, m_e, a_e, a_p):
        part = qk(qts, t + 2, s_even)
        expo(s_odd, p_odd, m_e)
        pv(p_even, t, a_p)
        m_2, a_2 = new_max(m_e, part)
        part = qk(qts, t + 3, s_odd)
        expo(s_even, p_even, m_2)
        pv(p_odd, t + 1, a_e)
        m_3, a_3 = new_max(m_2, part)
        return tuple(m_3), tuple(a_3), tuple(a_2)

    def block(qb, carry):
        qts = masked_q(qb)
        for i in range(n_chunks // 2 - 1):
            carry = steps(qts, 2 * i, *carry)
        return handover(qb, carry)

    assert n_chunks % 2 == 0 and n_chunks >= 4
    lax.fori_loop(0, n_q, block, handover(-1, None))


def _df_attention(qt, k, vt, k_meta, vt_meta, lam_vecs, subln_col):
    b, n_chunks, _, _ = qt.shape
    k4 = k.reshape(b, n_chunks, CHUNK, DIFF_WIDTH)
    return pl.pallas_call(
        functools.partial(_df_kernel, n_chunks=n_chunks),
        grid=(b, DIFF_HEADS),
        in_specs=[
            pl.BlockSpec((1, n_chunks, 128, CHUNK), lambda bi, h: (bi, 0, h, 0)),
            pl.BlockSpec((1, n_chunks, CHUNK, 128), lambda bi, h: (bi, 0, 0, h)),
            pl.BlockSpec((1, n_chunks, 1, V_ROWS, CHUNK), lambda bi, h: (bi, 0, h, 0, 0)),
            pl.BlockSpec((N_META, 128), lambda bi, h: (0, h)),
            pl.BlockSpec((1, V_ROWS, N_META), lambda bi, h: (h, 0, 0)),
            pl.BlockSpec((4, DIFF_HEAD_DIM), lambda bi, h: (0, 0)),
            pl.BlockSpec((128, 1), lambda bi, h: (0, 0)),
        ],
        out_specs=pl.BlockSpec((1, n_chunks, 128, CHUNK), lambda bi, h: (bi, 0, h, 0)),
        out_shape=jax.ShapeDtypeStruct(qt.shape, BF16),
        scratch_shapes=[pltpu.VMEM((CHUNK, CHUNK), F32)] * 4
        + [pltpu.VMEM((CHUNK, CHUNK), BF16)] * 4
        + [pltpu.VMEM((V_ROWS, CHUNK), F32)] * 2,
        compiler_params=pltpu.CompilerParams(
            dimension_semantics=("parallel", "parallel"), vmem_limit_bytes=VMEM_LIMIT_BYTES),
        name="df_attn",
    )(qt, k4, vt, k_meta, vt_meta, lam_vecs, subln_col)


def _out_ffn_kernel(x_ref, nat_ref, dft_ref, g_na_ref, g_df_ref, w_na_ref, w_df_ref, w_o_ref,
                    ffn_g_ref, w_gate_ref, w_up_ref, w_down_ref, fin_g_ref, o_ref):
    o_na = _dot_tn(nat_ref[0, 0], w_na_ref[...])
    o_df = _dot_tn(dft_ref[0, 0], w_df_ref[...])
    merged = g_na_ref[0].astype(F32) * o_na + g_df_ref[0].astype(F32) * o_df
    x1 = x_ref[0] + _dot(merged.astype(BF16), w_o_ref[...])
    h = (_rms_scale(x1, NORM_EPS) * ffn_g_ref[...]).astype(BF16)
    gate = _dot(h, w_gate_ref[...])
    up = _dot(h, w_up_ref[...])
    act = (gate * jax.nn.sigmoid(gate) * up).astype(BF16)
    x2 = x1 + _dot(act, w_down_ref[...])
    o_ref[0] = _rms_scale(x2, NORM_EPS) * fin_g_ref[...]


def _out_ffn(x, nat, dft, g_na, g_df, w_na, w_df, w_o, ffn_g, w_gate, w_up, w_down, fin_g, rows):
    b, t, _ = x.shape
    per_chunk = CHUNK // rows
    const = lambda shape: pl.BlockSpec(shape, lambda i, j: (0,) * len(shape),
                                       pipeline_mode=pl.Buffered(1))
    tok_spec = pl.BlockSpec((1, rows, D_MODEL), lambda i, j: (i, j, 0))
    feat_spec = pl.BlockSpec((1, 1, 512, rows), lambda i, j: (i, j // per_chunk, 0, j % per_chunk))
    return pl.pallas_call(
        _out_ffn_kernel,
        grid=(b, t // rows),
        in_specs=[tok_spec, feat_spec, feat_spec, tok_spec, tok_spec,
                  const((NA_WIDTH, D_MODEL)), const((DIFF_WIDTH, D_MODEL)), const((D_MODEL, D_MODEL)),
                  const((1, D_MODEL)), const((D_MODEL, D_FF)), const((D_MODEL, D_FF)),
                  const((D_FF, D_MODEL)), const((1, D_MODEL))],
        out_specs=tok_spec,
        out_shape=jax.ShapeDtypeStruct(x.shape, F32),
        compiler_params=pltpu.CompilerParams(
            dimension_semantics=("parallel", "parallel"), vmem_limit_bytes=VMEM_LIMIT_BYTES),
        name="out_ffn",
    )(x, nat, dft, g_na, g_df, w_na, w_df, w_o, ffn_g, w_gate, w_up, w_down, fin_g)


def kernel(x, meta_tokens, mix_norm, w_in, na_rpb, lambda_q1, lambda_k1, lambda_q2, lambda_k2,
           diff_subln, w_na_out, w_diff_out, w_o, ffn_norm, w_gate, w_up, w_down, final_norm):
    b, t, _ = x.shape
    assert mix_norm.shape[0] == 1, "single-layer block"
    assert t % CHUNK == 0 and t // CHUNK >= 3

    w = w_in[0].astype(BF16)
    na_q, na_k, na_v, df_q, df_k, df_v, g_na, g_df = jnp.split(
        w, [512, 1024, 1536, 2048, 2560, 3072, 4096], axis=1)
    w_t = jnp.concatenate([na_q, na_v, df_q, df_v], axis=1).T
    w_s = jnp.concatenate([na_k, df_k, g_na, g_df], axis=1)
    gain = mix_norm[0][None].astype(F32)

    pos = jnp.arange(N_META + t, dtype=jnp.int32)
    tables_meta = _rope_tables(pos[:N_META])
    tables_real = _rope_tables(pos[N_META:])

    na_qt, na_vt, df_qt, df_vt, na_k_r, df_k_r, g_na_r, g_df_r = _in_proj(
        x, gain, w_t, w_s, *tables_real, rows=2 * CHUNK)
    meta = _in_proj(meta_tokens[None].astype(x.dtype), gain, w_t, w_s, *tables_meta, rows=N_META)
    na_vt_m, df_vt_m, na_k_m, df_k_m = meta[1][0], meta[3][0, 0], meta[4][0], meta[5][0]

    bias = _na_bias_tiles(na_rpb[0], t // GRID_W)
    na_out_t = _na_attention(na_qt, na_k_r, na_vt, na_k_m, na_vt_m, bias)

    lam_vecs = jnp.stack([lambda_q1[0], lambda_k1[0], lambda_q2[0], lambda_k2[0]]).astype(F32)
    df_out_t = _df_attention(df_qt, df_k_r, df_vt, df_k_m, df_vt_m, lam_vecs,
                             diff_subln[0].astype(F32)[:, None])

    return _out_ffn(x, na_out_t, df_out_t, g_na_r, g_df_r,
                    w_na_out[0].astype(BF16), w_diff_out[0].astype(BF16), w_o[0].astype(BF16),
                    ffn_norm[0][None].astype(F32), w_gate[0].astype(BF16), w_up[0].astype(BF16),
                    w_down[0].astype(BF16), final_norm[None].astype(F32), rows=512)
```

```python
import functools
import math

import jax
import jax.numpy as jnp
from jax import lax
from jax.experimental import pallas as pl
from jax.experimental.pallas import tpu as pltpu

D_MODEL = 1024
GRID_W = 64
N_META = 16
NA_HEADS = 8
NA_HEAD_DIM = 64
NA_WIN_ROWS = 8
NA_WIN_COLS = 16
DIFF_HEADS = 4
DIFF_HEAD_DIM = 64
NA_WIDTH = NA_HEADS * NA_HEAD_DIM
DIFF_WIDTH = DIFF_HEADS * 2 * DIFF_HEAD_DIM
D_FF = 2816
ROPE_THETA = 10000.0
NORM_EPS = 1e-6
SUBLN_EPS = 1e-5
LAMBDA_INIT = 0.8 - 0.6 * math.exp(-0.3 * 0)

CHUNK = 512
NA_GROUP_ROWS = CHUNK // GRID_W
NA_KEY_ROWS = 16
NA_PAIR = 2 * GRID_W
NA_PAIR_KEY_ROWS = NA_WIN_ROWS + 2
NA_QUAD = 4
NA_QUAD_DIM = NA_QUAD * NA_HEAD_DIM
V_ROWS = 128 + 16
LOG2_E = math.log2(math.e)
MASK_VALUE = -1e30
VMEM_LIMIT_BYTES = 56 * 1024 * 1024

BF16 = jnp.bfloat16
F32 = jnp.float32


def _dot(a, b):
    return jnp.dot(a, b, preferred_element_type=F32)


def _dot_nt(a, b):
    return lax.dot_general(a, b, (((1,), (1,)), ((), ())), preferred_element_type=F32)


def _dot_tn(a, b):
    return lax.dot_general(a, b, (((0,), (0,)), ((), ())), preferred_element_type=F32)


def _rms_scale(x, eps):
    return x * lax.rsqrt(jnp.mean(x * x, axis=-1, keepdims=True) + eps)


def _in_proj_kernel(x_ref, g_ref, wt_ref, ws_ref, cos_t_ref, sin_t_ref, cos_s_ref, sin_s_ref,
                    na_qt_ref, na_vt_ref, df_qt_ref, df_vt_ref, na_k_ref, df_k_ref,
                    g_na_ref, g_df_ref):
    h = (_rms_scale(x_ref[0], NORM_EPS) * g_ref[...]).astype(BF16)

    g_na_ref[0] = jax.nn.sigmoid(_dot(h, ws_ref[:, 1024:2048])).astype(BF16)
    g_df_ref[0] = jax.nn.sigmoid(_dot(h, ws_ref[:, 2048:3072])).astype(BF16)

    n_ch, chunk = na_qt_ref.shape[1], na_qt_ref.shape[-1]
    lanes = [slice(ci * chunk, (ci + 1) * chunk) for ci in range(n_ch)]
    nq = (_dot_nt(wt_ref[0:512], h) * (LOG2_E * NA_HEAD_DIM ** -0.5)).astype(BF16)
    for ci in range(n_ch):
        na_qt_ref[0, ci] = nq[:, lanes[ci]]
    nv = _dot_nt(wt_ref[512:1024], h).astype(BF16)
    dv = _dot_nt(wt_ref[1536:2048], h).astype(BF16)
    pad_row = lax.broadcasted_iota(jnp.int32, (V_ROWS - 128, chunk), 0)
    ones_pad = jnp.where(pad_row == 0, 1.0, 0.0).astype(BF16)
    for ci in range(n_ch):
        for hd in range(DIFF_HEADS):
            df_vt_ref[0, ci, hd, 0:128, :] = dv[hd * 128:(hd + 1) * 128, lanes[ci]]
            df_vt_ref[0, ci, hd, 128:V_ROWS, :] = ones_pad
    pair = na_vt_ref.shape[-1]
    pad_row = lax.broadcasted_iota(jnp.int32, (V_ROWS - 128, pair), 0)
    ones_pair = jnp.where(pad_row == 0, 1.0, 0.0).astype(BF16)
    for hp in range(NA_HEADS // 2):
        for pr in range(nv.shape[1] // pair):
            na_vt_ref[0, hp, pr, 0:128, :] = nv[hp * 128:(hp + 1) * 128, pr * pair:(pr + 1) * pair]
            na_vt_ref[0, hp, pr, 128:V_ROWS, :] = ones_pair
    dq = _dot_nt(wt_ref[1024:1536], h) * (LOG2_E * DIFF_HEAD_DIM ** -0.5)
    cos_t = cos_t_ref[...]
    sin_t = sin_t_ref[...]
    half = DIFF_HEAD_DIM // 2
    for c in range(DIFF_WIDTH // DIFF_HEAD_DIM):
        x1 = dq[c * 64:c * 64 + half]
        x2 = dq[c * 64 + half:(c + 1) * 64]
        top = (x1 * cos_t - x2 * sin_t).astype(BF16)
        bottom = (x2 * cos_t + x1 * sin_t).astype(BF16)
        for ci in range(n_ch):
            df_qt_ref[0, ci, c * 64:c * 64 + half, :] = top[:, lanes[ci]]
            df_qt_ref[0, ci, c * 64 + half:(c + 1) * 64, :] = bottom[:, lanes[ci]]

    dk = _dot(h, ws_ref[:, 512:1024])
    cos_s = cos_s_ref[...]
    sin_s = sin_s_ref[...]
    lane = lax.broadcasted_iota(jnp.int32, cos_s.shape, 1)
    low_half = (lane % DIFF_HEAD_DIM) < half
    for c in range(DIFF_WIDTH // 128):
        xk = dk[:, c * 128:(c + 1) * 128]
        partner = jnp.where(low_half, pltpu.roll(xk, 128 - half, 1), pltpu.roll(xk, half, 1))
        df_k_ref[0, :, c * 128:(c + 1) * 128] = (xk * cos_s + partner * sin_s).astype(BF16)
    na_k_ref[0] = _dot(h, ws_ref[:, 0:512]).astype(BF16)


def _in_proj(x, gain, w_t, w_s, cos_t, sin_t, cos_s, sin_s, rows):
    b, t, _ = x.shape
    n = t // rows
    const = lambda shape: pl.BlockSpec(shape, lambda i, j: (0,) * len(shape),
                                       pipeline_mode=pl.Buffered(1))
    chunk = min(CHUNK, rows)
    n_ch = rows // chunk
    feat_major = jax.ShapeDtypeStruct((b, t // chunk, 512, chunk), BF16)
    feat_spec = pl.BlockSpec((1, n_ch, 512, chunk), lambda i, j: (i, j, 0, 0))
    pair = min(NA_PAIR, rows)
    value_major = jax.ShapeDtypeStruct((b, t // chunk, 4, V_ROWS, chunk), BF16)
    value_spec = pl.BlockSpec((1, n_ch, 4, V_ROWS, chunk), lambda i, j: (i, j, 0, 0, 0))
    tok = lambda width: jax.ShapeDtypeStruct((b, t, width), BF16)
    tok_spec = lambda width: pl.BlockSpec((1, rows, width), lambda i, j: (i, j, 0))
    return pl.pallas_call(
        _in_proj_kernel,
        grid=(b, n),
        in_specs=[
            pl.BlockSpec((1, rows, D_MODEL), lambda i, j: (i, j, 0)),
            const((1, D_MODEL)),
            const((2048, D_MODEL)),
            const((D_MODEL, 3072)),
            pl.BlockSpec((DIFF_HEAD_DIM // 2, rows), lambda i, j: (0, j)),
            pl.BlockSpec((DIFF_HEAD_DIM // 2, rows), lambda i, j: (0, j)),
            pl.BlockSpec((rows, 128), lambda i, j: (j, 0)),
            pl.BlockSpec((rows, 128), lambda i, j: (j, 0)),
        ],
        out_specs=[feat_spec,
                   pl.BlockSpec((1, NA_HEADS // 2, rows // pair, V_ROWS, pair),
                                lambda i, j: (i, 0, j, 0, 0)),
                   feat_spec, value_spec,
                   tok_spec(512), tok_spec(512), tok_spec(D_MODEL), tok_spec(D_MODEL)],
        out_shape=[feat_major,
                   jax.ShapeDtypeStruct((b, NA_HEADS // 2, t // pair, V_ROWS, pair), BF16),
                   feat_major, value_major,
                   tok(512), tok(512), tok(D_MODEL), tok(D_MODEL)],
        compiler_params=pltpu.CompilerParams(
            dimension_semantics=("parallel", "parallel"), vmem_limit_bytes=VMEM_LIMIT_BYTES),
        name="in_proj",
    )(x, gain, w_t, w_s, cos_t, sin_t, cos_s, sin_s)


def _rope_tables(pos):
    half = DIFF_HEAD_DIM // 2
    inv = ROPE_THETA ** (-jnp.arange(half, dtype=F32) / half)
    ang = pos.astype(F32)[:, None] * inv[None, :]
    cos, sin = jnp.cos(ang), jnp.sin(ang)
    cos_s = jnp.tile(cos, (1, 128 // half))
    sin_s = jnp.tile(jnp.concatenate([-sin, sin], axis=1), (1, 128 // DIFF_HEAD_DIM))
    return cos.T, sin.T, cos_s, sin_s


def _na_pair_base(first_row, n_rows):
    return jnp.clip(first_row - NA_WIN_ROWS // 2, 0, n_rows - NA_PAIR_KEY_ROWS)


def _na_pair_variants(n_rows):
    assert n_rows >= 2 * NA_PAIR_KEY_ROWS + 2
    kinds = []
    for first_row in (2 * NA_WIN_ROWS, 0, 2, n_rows - 4, n_rows - 2):
        base = min(max(first_row - NA_WIN_ROWS // 2, 0), n_rows - NA_PAIR_KEY_ROWS)
        rows = []
        for r in (first_row, first_row + 1):
            start = min(max(r - NA_WIN_ROWS // 2, 0), n_rows - NA_WIN_ROWS)
            rows.append((start - base, start - base + NA_WIN_ROWS, base - r))
        kinds.append(rows)
    return kinds


def _na_pair_kind(pair_index, n_pairs):
    return jnp.where(pair_index == 0, 1, jnp.where(pair_index == 1, 2, jnp.where(
        pair_index == n_pairs - 2, 3, jnp.where(pair_index == n_pairs - 1, 4, 0))))


def _na_bias_tiles(rpb, n_rows):
    n_heads, n_dr, n_dc = rpb.shape
    kinds = _na_pair_variants(n_rows)
    mid = NA_WIN_COLS - 1
    ring = jnp.concatenate([rpb[:, :, mid::-1].astype(F32),
                            jnp.zeros((n_heads, n_dr, 128 - n_dc), F32),
                            rpb[:, :, :mid:-1].astype(F32)], axis=-1)

    def body(ring_ref, o_ref):
        key_col = lax.broadcasted_iota(jnp.int32, (GRID_W, 128), 0)
        lane = lax.broadcasted_iota(jnp.int32, (GRID_W, 128), 1)
        col_start = jnp.clip(lane % GRID_W - NA_WIN_COLS // 2, 0, GRID_W - NA_WIN_COLS)
        col_ok = (key_col >= col_start) & (key_col < col_start + NA_WIN_COLS)
        masked = jnp.full((GRID_W, 128), MASK_VALUE, F32)
        tables = []
        for d in range(n_dr):
            x = jnp.broadcast_to(ring_ref[0, d:d + 1, :], (GRID_W, 128)) * LOG2_E
            for bit in range(GRID_W.bit_length() - 1):
                x = jnp.where((key_col >> bit) & 1 == 1, pltpu.roll(x, 1 << bit, 1), x)
            x = jnp.where(lane < GRID_W, x, pltpu.roll(x, GRID_W, 1))
            tables.append(jnp.where(col_ok, x, masked))
        for v, kind in enumerate(kinds):
            for i in range(NA_PAIR_KEY_ROWS):
                halves = [tables[shift + i + NA_WIN_ROWS - 1] if lo <= i < hi else masked
                          for lo, hi, shift in kind]
                o_ref[v, 0, i * GRID_W:(i + 1) * GRID_W, :] = jnp.where(
                    lane < GRID_W, halves[0], halves[1])

    n_keys = NA_PAIR_KEY_ROWS * GRID_W
    return pl.pallas_call(
        body,
        grid=(n_heads,),
        in_specs=[pl.BlockSpec((1, n_dr, 128), lambda h: (h, 0, 0))],
        out_specs=pl.BlockSpec((len(kinds), 1, n_keys, NA_PAIR),
                               lambda h: (0, h // NA_QUAD, 0, h % NA_QUAD)),
        out_shape=jax.ShapeDtypeStruct((len(kinds), n_heads // NA_QUAD, n_keys, NA_QUAD * NA_PAIR),
                                       F32),
        compiler_params=pltpu.CompilerParams(
            dimension_semantics=("parallel",), vmem_limit_bytes=VMEM_LIMIT_BYTES),
        name="na_bias",
    )(ring)


def _na_window_row(g, n_rows):
    quarter = NA_KEY_ROWS // 4
    return jnp.clip(2 * g - 1, 0, (n_rows - NA_KEY_ROWS) // quarter) * quarter


def _na_kernel(qt_ref, k_ref, vt_ref, km_ref, vm_ref, bias_ref, o_ref, *s_refs, n_rows):
    g = pl.program_id(1)
    n_pairs = n_rows // 2
    window_row = _na_window_row(g, n_rows)
    n_keys = NA_PAIR_KEY_ROWS * GRID_W
    head_of_row = lax.broadcasted_iota(jnp.int32, (NA_QUAD_DIM, NA_PAIR), 0) // NA_HEAD_DIM
    n_quads = NA_HEADS // NA_QUAD
    units = [(jp, quad) for jp in range(NA_GROUP_ROWS // 2) for quad in range(n_quads)]

    def local_row(jp):
        return _na_pair_base(2 * (g * (NA_GROUP_ROWS // 2) + jp), n_rows) - window_row

    def scores(u):
        jp, quad = units[u]
        kind = _na_pair_kind(g * (NA_GROUP_ROWS // 2) + jp, n_pairs)
        key_off = pl.multiple_of(local_row(jp) * GRID_W, NA_PAIR)
        halves, meta_halves = [], []
        for hp in range(NA_QUAD // 2):
            feat = slice(quad * NA_QUAD_DIM + hp * 128, quad * NA_QUAD_DIM + (hp + 1) * 128)
            q_pair = qt_ref[0, 0, feat, jp * NA_PAIR:(jp + 1) * NA_PAIR]
            zero = jnp.zeros_like(q_pair)
            w = jnp.concatenate([jnp.where(head_of_row[0:128] == hd, q_pair, zero) for hd in range(2)],
                                axis=1)
            halves.append(_dot(k_ref[0, pl.ds(key_off, n_keys), feat], w))
            meta_halves.append(_dot(km_ref[:, feat], w))
        s = jnp.concatenate(halves, axis=1) + bias_ref[kind, quad]
        s_refs[u][...] = s
        s_meta = jnp.concatenate(meta_halves, axis=1)
        m = jnp.maximum(jnp.max(_col_part(s, jnp.max), axis=0, keepdims=True),
                        jnp.max(s_meta, axis=0, keepdims=True))
        return s_meta, m

    def values(u, s_meta, m):
        jp, quad = units[u]
        slab_off = local_row(jp) // 2
        for hp in range(NA_QUAD // 2):
            slab = (NA_QUAD // 2) * quad + hp
            vt = jnp.concatenate([vt_ref[0, slab, slab_off + i] for i in range(NA_PAIR_KEY_ROWS // 2)],
                                 axis=1)
            cols = slice(2 * hp * NA_PAIR, 2 * (hp + 1) * NA_PAIR)
            p = jnp.exp2((s_refs[u][:, cols] - m[:, cols]).astype(BF16))
            p_meta = jnp.exp2((s_meta[:, cols] - m[:, cols]).astype(BF16))
            o = _dot(vt, p) + _dot(vm_ref[slab, 0], p_meta)
            denom = o[128:129]
            for hd in range(2):
                first = quad * NA_QUAD_DIM + (2 * hp + hd) * NA_HEAD_DIM
                o_ref[0, 0, first:first + NA_HEAD_DIM, jp * NA_PAIR:(jp + 1) * NA_PAIR] = (
                    o[hd * NA_HEAD_DIM:(hd + 1) * NA_HEAD_DIM, hd * NA_PAIR:(hd + 1) * NA_PAIR]
                    / denom[:, hd * NA_PAIR:(hd + 1) * NA_PAIR]).astype(BF16)

    pending = scores(0)
    for u in range(len(units)):
        following = scores(u + 1) if u + 1 < len(units) else None
        values(u, *pending)
        pending = following


def _na_attention(qt, k, vt, k_meta, vt_meta, bias):
    b, n_groups, _, _ = qt.shape
    n_rows = n_groups * NA_GROUP_ROWS
    n_quads = NA_HEADS // NA_QUAD
    n_kinds, _, n_keys, _ = bias.shape
    return pl.pallas_call(
        functools.partial(_na_kernel, n_rows=n_rows),
        grid=(b, n_groups),
        in_specs=[
            pl.BlockSpec((1, 1, NA_WIDTH, CHUNK), lambda bi, g: (bi, g, 0, 0)),
            pl.BlockSpec((pl.Element(1), pl.Element(NA_KEY_ROWS * GRID_W), pl.Element(NA_WIDTH)),
                         lambda bi, g: (bi, _na_window_row(g, n_rows) * GRID_W, 0)),
            pl.BlockSpec((pl.Element(1), pl.Element(NA_HEADS // 2), pl.Element(NA_KEY_ROWS // 2),
                          pl.Element(V_ROWS), pl.Element(NA_PAIR)),
                         lambda bi, g: (bi, 0, _na_window_row(g, n_rows) // 2, 0, 0)),
            pl.BlockSpec((N_META, NA_WIDTH), lambda bi, g: (0, 0)),
            pl.BlockSpec((NA_HEADS // 2, 1, V_ROWS, N_META), lambda bi, g: (0, 0, 0, 0)),
            pl.BlockSpec((n_kinds, n_quads, n_keys, NA_QUAD * NA_PAIR), lambda bi, g: (0, 0, 0, 0),
                         pipeline_mode=pl.Buffered(1)),
        ],
        out_specs=pl.BlockSpec((1, 1, NA_WIDTH, CHUNK), lambda bi, g: (bi, g, 0, 0)),
        out_shape=jax.ShapeDtypeStruct(qt.shape, BF16),
        scratch_shapes=[pltpu.VMEM((n_keys, NA_QUAD * NA_PAIR), F32)]
        * (n_quads * NA_GROUP_ROWS // 2),
        compiler_params=pltpu.CompilerParams(
            dimension_semantics=("parallel", "parallel"), vmem_limit_bytes=VMEM_LIMIT_BYTES),
        name="na_attn",
    )(qt, k, vt, k_meta, vt_meta, bias)


def _col_part(x, op):
    return op(x.reshape(x.shape[0] // 8, 8, x.shape[1]), axis=0)


def _df_kernel(qt_ref, k_ref, vt_ref, km_ref, vm_ref, lam_ref, g_ref, o_ref,
               s_a0, s_a1, s_b0, s_b1, p_a0, p_a1, p_b0, p_b1, acc0_ref, acc1_ref, *, n_chunks):
    s_even, s_odd = (s_a0, s_a1), (s_b0, s_b1)
    p_even, p_odd = (p_a0, p_a1), (p_b0, p_b1)
    acc_refs = (acc0_ref, acc1_ref)
    n_q = n_chunks
    k_meta = km_ref[...]
    vt_meta = vm_ref[0]
    lam_v = lam_ref[...]
    lam = (jnp.exp(jnp.sum(lam_v[0:1] * lam_v[1:2], axis=1, keepdims=True))
           - jnp.exp(jnp.sum(lam_v[2:3] * lam_v[3:4], axis=1, keepdims=True)) + LAMBDA_INIT)

    def masked_q(qb):
        qt = qt_ref[0, qb]
        row = lax.broadcasted_iota(jnp.int32, qt.shape, 0)
        zero = jnp.zeros_like(qt)
        return (jnp.where(row < DIFF_HEAD_DIM, qt, zero), jnp.where(row >= DIFF_HEAD_DIM, qt, zero))

    def qk(qts, chunk, s_bufs):
        kc = k_ref[0, chunk]
        part = []
        for c in range(2):
            dims = slice(c * DIFF_HEAD_DIM, (c + 1) * DIFF_HEAD_DIM)
            s = _dot(kc[:, dims], qts[c][dims, :])
            s_bufs[c][...] = s
            part.append(_col_part(s, jnp.max))
        return part

    def new_max(m, part):
        m_new = [jnp.maximum(m[c], jnp.max(part[c], axis=0, keepdims=True)) for c in range(2)]
        return m_new, [jnp.exp2(m[c] - m_new[c]) for c in range(2)]

    def expo(s_bufs, p_bufs, m):
        for c in range(2):
            p_bufs[c][...] = jnp.exp2((s_bufs[c][...] - m[c]).astype(BF16))

    def pv(p_bufs, chunk, alpha):
        vc = vt_ref[0, chunk, 0]
        for c in range(2):
            acc_refs[c][...] = acc_refs[c][...] * alpha[c] + _dot(vc, p_bufs[c][...])

    def finish(qb):
        o = (acc0_ref[0:128] / acc0_ref[128:129]
             - lam * (acc1_ref[0:128] / acc1_ref[128:129]))
        o = o * lax.rsqrt(jnp.mean(o * o, axis=0, keepdims=True) + SUBLN_EPS)
        o_ref[0, qb] = (o * g_ref[...] * (1.0 - LAMBDA_INIT)).astype(BF16)

    def handover(qb, cur):
        nxt = jnp.minimum(qb + 1, n_q - 1)
        qts = masked_q(nxt)
        part0 = qk(qts, 0, s_even)
        s_meta = [_dot(k_meta, qts[c]) for c in range(2)]
        if cur is not None:
            expo(s_odd, p_odd, cur[0])
            pv(p_even, n_chunks - 2, cur[2])
        m0 = [jnp.maximum(jnp.max(part0[c], axis=0, keepdims=True),
                          jnp.max(s_meta[c], axis=0, keepdims=True)) for c in range(2)]
        part1 = qk(qts, 1, s_odd)
        expo(s_even, p_even, m0)
        if cur is not None:
            pv(p_odd, n_chunks - 1, cur[1])
            finish(qb)
        for c in range(2):
            acc_refs[c][...] = _dot(vt_meta, jnp.exp2((s_meta[c] - m0[c]).astype(BF16)))
        m1, a1 = new_max(m0, part1)
        return tuple(m1), tuple(a1), tuple(jnp.ones_like(x) for x in m1)

    def steps(qts, t, m_e, a_e, a_p):
        part = qk(qts, t + 2, s_even)
        expo(s_odd, p_odd, m_e)
        pv(p_even, t, a_p)
        m_2, a_2 = new_max(m_e, part)
        part = qk(qts, t + 3, s_odd)
        expo(s_even, p_even, m_2)
        pv(p_odd, t + 1, a_e)
        m_3, a_3 = new_max(m_2, part)
        return tuple(m_3), tuple(a_3), tuple(a_2)

    def block(qb, carry):
        qts = masked_q(qb)
        for i in range(n_chunks // 2 - 1):
            carry = steps(qts, 2 * i, *carry)
        return handover(qb, carry)

    assert n_chunks % 2 == 0 and n_chunks >= 4
    lax.fori_loop(0, n_q, block, handover(-1, None))


def _df_attention(qt, k, vt, k_meta, vt_meta, lam_vecs, subln_col):
    b, n_chunks, _, _ = qt.shape
    k4 = k.reshape(b, n_chunks, CHUNK, DIFF_WIDTH)
    return pl.pallas_call(
        functools.partial(_df_kernel, n_chunks=n_chunks),
        grid=(b, DIFF_HEADS),
        in_specs=[
            pl.BlockSpec((1, n_chunks, 128, CHUNK), lambda bi, h: (bi, 0, h, 0)),
            pl.BlockSpec((1, n_chunks, CHUNK, 128), lambda bi, h: (bi, 0, 0, h)),
            pl.BlockSpec((1, n_chunks, 1, V_ROWS, CHUNK), lambda bi, h: (bi, 0, h, 0, 0)),
            pl.BlockSpec((N_META, 128), lambda bi, h: (0, h)),
            pl.BlockSpec((1, V_ROWS, N_META), lambda bi, h: (h, 0, 0)),
            pl.BlockSpec((4, DIFF_HEAD_DIM), lambda bi, h: (0, 0)),
            pl.BlockSpec((128, 1), lambda bi, h: (0, 0)),
        ],
        out_specs=pl.BlockSpec((1, n_chunks, 128, CHUNK), lambda bi, h: (bi, 0, h, 0)),
        out_shape=jax.ShapeDtypeStruct(qt.shape, BF16),
        scratch_shapes=[pltpu.VMEM((CHUNK, CHUNK), F32)] * 4
        + [pltpu.VMEM((CHUNK, CHUNK), BF16)] * 4
        + [pltpu.VMEM((V_ROWS, CHUNK), F32)] * 2,
        compiler_params=pltpu.CompilerParams(
            dimension_semantics=("parallel", "parallel"), vmem_limit_bytes=VMEM_LIMIT_BYTES),
        name="df_attn",
    )(qt, k4, vt, k_meta, vt_meta, lam_vecs, subln_col)


def _out_ffn_kernel(x_ref, nat_ref, dft_ref, g_na_ref, g_df_ref, w_na_ref, w_df_ref, w_o_ref,
                    ffn_g_ref, w_gate_ref, w_up_ref, w_down_ref, fin_g_ref, o_ref):
    o_na = _dot_tn(nat_ref[0, 0], w_na_ref[...])
    o_df = _dot_tn(dft_ref[0, 0], w_df_ref[...])
    merged = g_na_ref[0].astype(F32) * o_na + g_df_ref[0].astype(F32) * o_df
    x1 = x_ref[0] + _dot(merged.astype(BF16), w_o_ref[...])
    h = (_rms_scale(x1, NORM_EPS) * ffn_g_ref[...]).astype(BF16)
    gate = _dot(h, w_gate_ref[...])
    up = _dot(h, w_up_ref[...])
    act = (gate * jax.nn.sigmoid(gate) * up).astype(BF16)
    x2 = x1 + _dot(act, w_down_ref[...])
    o_ref[0] = _rms_scale(x2, NORM_EPS) * fin_g_ref[...]


def _out_ffn(x, nat, dft, g_na, g_df, w_na, w_df, w_o, ffn_g, w_gate, w_up, w_down, fin_g, rows):
    b, t, _ = x.shape
    per_chunk = CHUNK // rows
    const = lambda shape: pl.BlockSpec(shape, lambda i, j: (0,) * len(shape),
                                       pipeline_mode=pl.Buffered(1))
    tok_spec = pl.BlockSpec((1, rows, D_MODEL), lambda i, j: (i, j, 0))
    feat_spec = pl.BlockSpec((1, 1, 512, rows), lambda i, j: (i, j // per_chunk, 0, j % per_chunk))
    return pl.pallas_call(
        _out_ffn_kernel,
        grid=(b, t // rows),
        in_specs=[tok_spec, feat_spec, feat_spec, tok_spec, tok_spec,
                  const((NA_WIDTH, D_MODEL)), const((DIFF_WIDTH, D_MODEL)), const((D_MODEL, D_MODEL)),
                  const((1, D_MODEL)), const((D_MODEL, D_FF)), const((D_MODEL, D_FF)),
                  const((D_FF, D_MODEL)), const((1, D_MODEL))],
        out_specs=tok_spec,
        out_shape=jax.ShapeDtypeStruct(x.shape, F32),
        compiler_params=pltpu.CompilerParams(
            dimension_semantics=("parallel", "parallel"), vmem_limit_bytes=VMEM_LIMIT_BYTES),
        name="out_ffn",
    )(x, nat, dft, g_na, g_df, w_na, w_df, w_o, ffn_g, w_gate, w_up, w_down, fin_g)


def kernel(x, meta_tokens, mix_norm, w_in, na_rpb, lambda_q1, lambda_k1, lambda_q2, lambda_k2,
           diff_subln, w_na_out, w_diff_out, w_o, ffn_norm, w_gate, w_up, w_down, final_norm):
    b, t, _ = x.shape
    assert mix_norm.shape[0] == 1, "single-layer block"
    assert t % CHUNK == 0 and t // CHUNK >= 3

    w = w_in[0].astype(BF16)
    na_q, na_k, na_v, df_q, df_k, df_v, g_na, g_df = jnp.split(
        w, [512, 1024, 1536, 2048, 2560, 3072, 4096], axis=1)
    w_t = jnp.concatenate([na_q, na_v, df_q, df_v], axis=1).T
    w_s = jnp.concatenate([na_k, df_k, g_na, g_df], axis=1)
    gain = mix_norm[0][None].astype(F32)

    pos = jnp.arange(N_META + t, dtype=jnp.int32)
    tables_meta = _rope_tables(pos[:N_META])
    tables_real = _rope_tables(pos[N_META:])

    na_qt, na_vt, df_qt, df_vt, na_k_r, df_k_r, g_na_r, g_df_r = _in_proj(
        x, gain, w_t, w_s, *tables_real, rows=2 * CHUNK)
    meta = _in_proj(meta_tokens[None].astype(x.dtype), gain, w_t, w_s, *tables_meta, rows=N_META)
    na_vt_m, df_vt_m, na_k_m, df_k_m = meta[1][0], meta[3][0, 0], meta[4][0], meta[5][0]

    bias = _na_bias_tiles(na_rpb[0], t // GRID_W)
    na_out_t = _na_attention(na_qt, na_k_r, na_vt, na_k_m, na_vt_m, bias)

    lam_vecs = jnp.stack([lambda_q1[0], lambda_k1[0], lambda_q2[0], lambda_k2[0]]).astype(F32)
    df_out_t = _df_attention(df_qt, df_k_r, df_vt, df_k_m, df_vt_m, lam_vecs,
                             diff_subln[0].astype(F32)[:, None])

    return _out_ffn(x, na_out_t, df_out_t, g_na_r, g_df_r,
                    w_na_out[0].astype(BF16), w_diff_out[0].astype(BF16), w_o[0].astype(BF16),
                    ffn_norm[0][None].astype(F32), w_gate[0].astype(BF16), w_up[0].astype(BF16),
                    w_down[0].astype(BF16), final_norm[None].astype(F32), rows=512)
```

```python
import functools
import math

import jax
import jax.numpy as jnp
from jax import lax
from jax.experimental import pallas as pl
from jax.experimental.pallas import tpu as pltpu

D_MODEL = 1024
GRID_W = 64
N_META = 16
NA_HEADS = 8
NA_HEAD_DIM = 64
NA_WIN_ROWS = 8
NA_WIN_COLS = 16
DIFF_HEADS = 4
DIFF_HEAD_DIM = 64
NA_WIDTH = NA_HEADS * NA_HEAD_DIM
DIFF_WIDTH = DIFF_HEADS * 2 * DIFF_HEAD_DIM
D_FF = 2816
ROPE_THETA = 10000.0
NORM_EPS = 1e-6
SUBLN_EPS = 1e-5
LAMBDA_INIT = 0.8 - 0.6 * math.exp(-0.3 * 0)

CHUNK = 512
NA_GROUP_ROWS = CHUNK // GRID_W
NA_KEY_ROWS = 16
NA_PAIR = 2 * GRID_W
NA_PAIR_KEY_ROWS = NA_WIN_ROWS + 2
NA_QUAD = 4
NA_QUAD_DIM = NA_QUAD * NA_HEAD_DIM
V_ROWS = 128 + 16
LOG2_E = math.log2(math.e)
MASK_VALUE = -1e30
VMEM_LIMIT_BYTES = 56 * 1024 * 1024

BF16 = jnp.bfloat16
F32 = jnp.float32


def _dot(a, b):
    return jnp.dot(a, b, preferred_element_type=F32)


def _dot_nt(a, b):
    return lax.dot_general(a, b, (((1,), (1,)), ((), ())), preferred_element_type=F32)


def _dot_tn(a, b):
    return lax.dot_general(a, b, (((0,), (0,)), ((), ())), preferred_element_type=F32)


def _rms_scale(x, eps):
    return x * lax.rsqrt(jnp.mean(x * x, axis=-1, keepdims=True) + eps)


def _in_proj_kernel(x_ref, g_ref, wt_ref, ws_ref, cos_t_ref, sin_t_ref, cos_s_ref, sin_s_ref,
                    na_qt_ref, na_vt_ref, df_qt_ref, df_vt_ref, na_k_ref, df_k_ref,
                    g_na_ref, g_df_ref):
    h = (_rms_scale(x_ref[0], NORM_EPS) * g_ref[...]).astype(BF16)

    g_na_ref[0] = jax.nn.sigmoid(_dot(h, ws_ref[:, 1024:2048])).astype(BF16)
    g_df_ref[0] = jax.nn.sigmoid(_dot(h, ws_ref[:, 2048:3072])).astype(BF16)

    n_ch, chunk = na_qt_ref.shape[1], na_qt_ref.shape[-1]
    lanes = [slice(ci * chunk, (ci + 1) * chunk) for ci in range(n_ch)]
    nq = (_dot_nt(wt_ref[0:512], h) * (LOG2_E * NA_HEAD_DIM ** -0.5)).astype(BF16)
    for ci in range(n_ch):
        na_qt_ref[0, ci] = nq[:, lanes[ci]]
    nv = _dot_nt(wt_ref[512:1024], h).astype(BF16)
    dv = _dot_nt(wt_ref[1536:2048], h).astype(BF16)
    pad_row = lax.broadcasted_iota(jnp.int32, (V_ROWS - 128, chunk), 0)
    ones_pad = jnp.where(pad_row == 0, 1.0, 0.0).astype(BF16)
    for ci in range(n_ch):
        for hd in range(DIFF_HEADS):
            df_vt_ref[0, ci, hd, 0:128, :] = dv[hd * 128:(hd + 1) * 128, lanes[ci]]
            df_vt_ref[0, ci, hd, 128:V_ROWS, :] = ones_pad
    pair = na_vt_ref.shape[-1]
    pad_row = lax.broadcasted_iota(jnp.int32, (V_ROWS - 128, pair), 0)
    ones_pair = jnp.where(pad_row == 0, 1.0, 0.0).astype(BF16)
    for hp in range(NA_HEADS // 2):
        for pr in range(nv.shape[1] // pair):
            na_vt_ref[0, hp, pr, 0:128, :] = nv[hp * 128:(hp + 1) * 128, pr * pair:(pr + 1) * pair]
            na_vt_ref[0, hp, pr, 128:V_ROWS, :] = ones_pair
    dq = _dot_nt(wt_ref[1024:1536], h) * (LOG2_E * DIFF_HEAD_DIM ** -0.5)
    cos_t = cos_t_ref[...]
    sin_t = sin_t_ref[...]
    half = DIFF_HEAD_DIM // 2
    for c in range(DIFF_WIDTH // DIFF_HEAD_DIM):
        x1 = dq[c * 64:c * 64 + half]
        x2 = dq[c * 64 + half:(c + 1) * 64]
        top = (x1 * cos_t - x2 * sin_t).astype(BF16)
        bottom = (x2 * cos_t + x1 * sin_t).astype(BF16)
        for ci in range(n_ch):
            df_qt_ref[0, ci, c * 64:c * 64 + half, :] = top[:, lanes[ci]]
            df_qt_ref[0, ci, c * 64 + half:(c + 1) * 64, :] = bottom[:, lanes[ci]]

    dk = _dot(h, ws_ref[:, 512:1024])
    cos_s = cos_s_ref[...]
    sin_s = sin_s_ref[...]
    lane = lax.broadcasted_iota(jnp.int32, cos_s.shape, 1)
    low_half = (lane % DIFF_HEAD_DIM) < half
    for c in range(DIFF_WIDTH // 128):
        xk = dk[:, c * 128:(c + 1) * 128]
        partner = jnp.where(low_half, pltpu.roll(xk, 128 - half, 1), pltpu.roll(xk, half, 1))
        df_k_ref[0, :, c * 128:(c + 1) * 128] = (xk * cos_s + partner * sin_s).astype(BF16)
    na_k_ref[0] = _dot(h, ws_ref[:, 0:512]).astype(BF16)


def _in_proj(x, gain, w_t, w_s, cos_t, sin_t, cos_s, sin_s, rows):
    b, t, _ = x.shape
    n = t // rows
    const = lambda shape: pl.BlockSpec(shape, lambda i, j: (0,) * len(shape),
                                       pipeline_mode=pl.Buffered(1))
    chunk = min(CHUNK, rows)
    n_ch = rows // chunk
    feat_major = jax.ShapeDtypeStruct((b, t // chunk, 512, chunk), BF16)
    feat_spec = pl.BlockSpec((1, n_ch, 512, chunk), lambda i, j: (i, j, 0, 0))
    pair = min(NA_PAIR, rows)
    value_major = jax.ShapeDtypeStruct((b, t // chunk, 4, V_ROWS, chunk), BF16)
    value_spec = pl.BlockSpec((1, n_ch, 4, V_ROWS, chunk), lambda i, j: (i, j, 0, 0, 0))
    tok = lambda width: jax.ShapeDtypeStruct((b, t, width), BF16)
    tok_spec = lambda width: pl.BlockSpec((1, rows, width), lambda i, j: (i, j, 0))
    return pl.pallas_call(
        _in_proj_kernel,
        grid=(b, n),
        in_specs=[
            pl.BlockSpec((1, rows, D_MODEL), lambda i, j: (i, j, 0)),
            const((1, D_MODEL)),
            const((2048, D_MODEL)),
            const((D_MODEL, 3072)),
            pl.BlockSpec((DIFF_HEAD_DIM // 2, rows), lambda i, j: (0, j)),
            pl.BlockSpec((DIFF_HEAD_DIM // 2, rows), lambda i, j: (0, j)),
            pl.BlockSpec((rows, 128), lambda i, j: (j, 0)),
            pl.BlockSpec((rows, 128), lambda i, j: (j, 0)),
        ],
        out_specs=[feat_spec,
                   pl.BlockSpec((1, NA_HEADS // 2, rows // pair, V_ROWS, pair),
                                lambda i, j: (i, 0, j, 0, 0)),
                   feat_spec, value_spec,
                   tok_spec(512), tok_spec(512), tok_spec(D_MODEL), tok_spec(D_MODEL)],
        out_shape=[feat_major,
                   jax.ShapeDtypeStruct((b, NA_HEADS // 2, t // pair, V_ROWS, pair), BF16),
                   feat_major, value_major,
                   tok(512), tok(512), tok(D_MODEL), tok(D_MODEL)],
        compiler_params=pltpu.CompilerParams(
            dimension_semantics=("parallel", "parallel"), vmem_limit_bytes=VMEM_LIMIT_BYTES),
        name="in_proj",
    )(x, gain, w_t, w_s, cos_t, sin_t, cos_s, sin_s)


def _rope_tables(pos):
    half = DIFF_HEAD_DIM // 2
    inv = ROPE_THETA ** (-jnp.arange(half, dtype=F32) / half)
    ang = pos.astype(F32)[:, None] * inv[None, :]
    cos, sin = jnp.cos(ang), jnp.sin(ang)
    cos_s = jnp.tile(cos, (1, 128 // half))
    sin_s = jnp.tile(jnp.concatenate([-sin, sin], axis=1), (1, 128 // DIFF_HEAD_DIM))
    return cos.T, sin.T, cos_s, sin_s


def _na_pair_base(first_row, n_rows):
    return jnp.clip(first_row - NA_WIN_ROWS // 2, 0, n_rows - NA_PAIR_KEY_ROWS)


def _na_pair_variants(n_rows):
    assert n_rows >= 2 * NA_PAIR_KEY_ROWS + 2
    kinds = []
    for first_row in (2 * NA_WIN_ROWS, 0, 2, n_rows - 4, n_rows - 2):
        base = min(max(first_row - NA_WIN_ROWS // 2, 0), n_rows - NA_PAIR_KEY_ROWS)
        rows = []
        for r in (first_row, first_row + 1):
            start = min(max(r - NA_WIN_ROWS // 2, 0), n_rows - NA_WIN_ROWS)
            rows.append((start - base, start - base + NA_WIN_ROWS, base - r))
        kinds.append(rows)
    return kinds


def _na_pair_kind(pair_index, n_pairs):
    return jnp.where(pair_index == 0, 1, jnp.where(pair_index == 1, 2, jnp.where(
        pair_index == n_pairs - 2, 3, jnp.where(pair_index == n_pairs - 1, 4, 0))))


def _na_bias_tiles(rpb, n_rows):
    n_heads, n_dr, n_dc = rpb.shape
    kinds = _na_pair_variants(n_rows)
    mid = NA_WIN_COLS - 1
    ring = jnp.concatenate([rpb[:, :, mid::-1].astype(F32),
                            jnp.zeros((n_heads, n_dr, 128 - n_dc), F32),
                            rpb[:, :, :mid:-1].astype(F32)], axis=-1)

    def body(ring_ref, o_ref):
        key_col = lax.broadcasted_iota(jnp.int32, (GRID_W, 128), 0)
        lane = lax.broadcasted_iota(jnp.int32, (GRID_W, 128), 1)
        col_start = jnp.clip(lane % GRID_W - NA_WIN_COLS // 2, 0, GRID_W - NA_WIN_COLS)
        col_ok = (key_col >= col_start) & (key_col < col_start + NA_WIN_COLS)
        masked = jnp.full((GRID_W, 128), MASK_VALUE, F32)
        tables = []
        for d in range(n_dr):
            x = jnp.broadcast_to(ring_ref[0, d:d + 1, :], (GRID_W, 128)) * LOG2_E
            for bit in range(GRID_W.bit_length() - 1):
                x = jnp.where((key_col >> bit) & 1 == 1, pltpu.roll(x, 1 << bit, 1), x)
            x = jnp.where(lane < GRID_W, x, pltpu.roll(x, GRID_W, 1))
            tables.append(jnp.where(col_ok, x, masked))
        for v, kind in enumerate(kinds):
            for i in range(NA_PAIR_KEY_ROWS):
                halves = [tables[shift + i + NA_WIN_ROWS - 1] if lo <= i < hi else masked
                          for lo, hi, shift in kind]
                o_ref[v, 0, i * GRID_W:(i + 1) * GRID_W, :] = jnp.where(
                    lane < GRID_W, halves[0], halves[1])

    n_keys = NA_PAIR_KEY_ROWS * GRID_W
    return pl.pallas_call(
        body,
        grid=(n_heads,),
        in_specs=[pl.BlockSpec((1, n_dr, 128), lambda h: (h, 0, 0))],
        out_specs=pl.BlockSpec((len(kinds), 1, n_keys, NA_PAIR),
                               lambda h: (0, h // NA_QUAD, 0, h % NA_QUAD)),
        out_shape=jax.ShapeDtypeStruct((len(kinds), n_heads // NA_QUAD, n_keys, NA_QUAD * NA_PAIR),
                                       F32),
        compiler_params=pltpu.CompilerParams(
            dimension_semantics=("parallel",), vmem_limit_bytes=VMEM_LIMIT_BYTES),
        name="na_bias",
    )(ring)


def _na_window_row(g, n_rows):
    quarter = NA_KEY_ROWS // 4
    return jnp.clip(2 * g - 1, 0, (n_rows - NA_KEY_ROWS) // quarter) * quarter


def _na_kernel(qt_ref, k_ref, vt_ref, km_ref, vm_ref, bias_ref, o_ref, *s_refs, n_rows):
    g = pl.program_id(1)
    n_pairs = n_rows // 2
    window_row = _na_window_row(g, n_rows)
    n_keys = NA_PAIR_KEY_ROWS * GRID_W
    head_of_row = lax.broadcasted_iota(jnp.int32, (NA_QUAD_DIM, NA_PAIR), 0) // NA_HEAD_DIM
    n_quads = NA_HEADS // NA_QUAD
    units = [(jp, quad) for jp in range(NA_GROUP_ROWS // 2) for quad in range(n_quads)]

    def local_row(jp):
        return _na_pair_base(2 * (g * (NA_GROUP_ROWS // 2) + jp), n_rows) - window_row

    def scores(u):
        jp, quad = units[u]
        kind = _na_pair_kind(g * (NA_GROUP_ROWS // 2) + jp, n_pairs)
        key_off = pl.multiple_of(local_row(jp) * GRID_W, NA_PAIR)
        halves, meta_halves = [], []
        for hp in range(NA_QUAD // 2):
            feat = slice(quad * NA_QUAD_DIM + hp * 128, quad * NA_QUAD_DIM + (hp + 1) * 128)
            q_pair = qt_ref[0, 0, feat, jp * NA_PAIR:(jp + 1) * NA_PAIR]
            zero = jnp.zeros_like(q_pair)
            w = jnp.concatenate([jnp.where(head_of_row[0:128] == hd, q_pair, zero) for hd in range(2)],
                                axis=1)
            halves.append(_dot(k_ref[0, pl.ds(key_off, n_keys), feat], w))
            meta_halves.append(_dot(km_ref[:, feat], w))
        s = jnp.concatenate(halves, axis=1) + bias_ref[kind, quad]
        s_refs[u][...] = s
        s_meta = jnp.concatenate(meta_halves, axis=1)
        m = jnp.maximum(jnp.max(_col_part(s, jnp.max), axis=0, keepdims=True),
                        jnp.max(s_meta, axis=0, keepdims=True))
        return s_meta, m

    def values(u, s_meta, m):
        jp, quad = units[u]
        slab_off = local_row(jp) // 2
        for hp in range(NA_QUAD // 2):
            slab = (NA_QUAD // 2) * quad + hp
            vt = jnp.concatenate([vt_ref[0, slab, slab_off + i] for i in range(NA_PAIR_KEY_ROWS // 2)],
                                 axis=1)
            cols = slice(2 * hp * NA_PAIR, 2 * (hp + 1) * NA_PAIR)
            p = jnp.exp2((s_refs[u][:, cols] - m[:, cols]).astype(BF16))
            p_meta = jnp.exp2((s_meta[:, cols] - m[:, cols]).astype(BF16))
            o = _dot(vt, p) + _dot(vm_ref[slab, 0], p_meta)
            denom = o[128:129]
            for hd in range(2):
                first = quad * NA_QUAD_DIM + (2 * hp + hd) * NA_HEAD_DIM
                o_ref[0, 0, first:first + NA_HEAD_DIM, jp * NA_PAIR:(jp + 1) * NA_PAIR] = (
                    o[hd * NA_HEAD_DIM:(hd + 1) * NA_HEAD_DIM, hd * NA_PAIR:(hd + 1) * NA_PAIR]
                    / denom[:, hd * NA_PAIR:(hd + 1) * NA_PAIR]).astype(BF16)

    pending = scores(0)
    for u in range(len(units)):
        following = scores(u + 1) if u + 1 < len(units) else None
        values(u, *pending)
        pending = following


def _na_attention(qt, k, vt, k_meta, vt_meta, bias):
    b, n_groups, _, _ = qt.shape
    n_rows = n_groups * NA_GROUP_ROWS
    n_quads = NA_HEADS // NA_QUAD
    n_kinds, _, n_keys, _ = bias.shape
    return pl.pallas_call(
        functools.partial(_na_kernel, n_rows=n_rows),
        grid=(b, n_groups),
        in_specs=[
            pl.BlockSpec((1, 1, NA_WIDTH, CHUNK), lambda bi, g: (bi, g, 0, 0)),
            pl.BlockSpec((pl.Element(1), pl.Element(NA_KEY_ROWS * GRID_W), pl.Element(NA_WIDTH)),
                         lambda bi, g: (bi, _na_window_row(g, n_rows) * GRID_W, 0)),
            pl.BlockSpec((pl.Element(1), pl.Element(NA_HEADS // 2), pl.Element(NA_KEY_ROWS // 2),
                          pl.Element(V_ROWS), pl.Element(NA_PAIR)),
                         lambda bi, g: (bi, 0, _na_window_row(g, n_rows) // 2, 0, 0)),
            pl.BlockSpec((N_META, NA_WIDTH), lambda bi, g: (0, 0)),
            pl.BlockSpec((NA_HEADS // 2, 1, V_ROWS, N_META), lambda bi, g: (0, 0, 0, 0)),
            pl.BlockSpec((n_kinds, n_quads, n_keys, NA_QUAD * NA_PAIR), lambda bi, g: (0, 0, 0, 0),
                         pipeline_mode=pl.Buffered(1)),
        ],
        out_specs=pl.BlockSpec((1, 1, NA_WIDTH, CHUNK), lambda bi, g: (bi, g, 0, 0)),
        out_shape=jax.ShapeDtypeStruct(qt.shape, BF16),
        scratch_shapes=[pltpu.VMEM((n_keys, NA_QUAD * NA_PAIR), F32)]
        * (n_quads * NA_GROUP_ROWS // 2),
        compiler_params=pltpu.CompilerParams(
            dimension_semantics=("parallel", "parallel"), vmem_limit_bytes=VMEM_LIMIT_BYTES),
        name="na_attn",
    )(qt, k, vt, k_meta, vt_meta, bias)


def _col_part(x, op):
    return op(x.reshape(x.shape[0] // 8, 8, x.shape[1]), axis=0)


def _df_kernel(qt_ref, k_ref, vt_ref, km_ref, vm_ref, lam_ref, g_ref, o_ref,
               s_a0, s_a1, s_b0, s_b1, p_a0, p_a1, p_b0, p_b1, acc0_ref, acc1_ref, *, n_chunks):
    s_even, s_odd = (s_a0, s_a1), (s_b0, s_b1)
    p_even, p_odd = (p_a0, p_a1), (p_b0, p_b1)
    acc_refs = (acc0_ref, acc1_ref)
    n_q = n_chunks
    k_meta = km_ref[...]
    vt_meta = vm_ref[0]
    lam_v = lam_ref[...]
    lam = (jnp.exp(jnp.sum(lam_v[0:1] * lam_v[1:2], axis=1, keepdims=True))
           - jnp.exp(jnp.sum(lam_v[2:3] * lam_v[3:4], axis=1, keepdims=True)) + LAMBDA_INIT)

    def masked_q(qb):
        qt = qt_ref[0, qb]
        row = lax.broadcasted_iota(jnp.int32, qt.shape, 0)
        zero = jnp.zeros_like(qt)
        return (jnp.where(row < DIFF_HEAD_DIM, qt, zero), jnp.where(row >= DIFF_HEAD_DIM, qt, zero))

    def qk(qts, chunk, s_bufs):
        kc = k_ref[0, chunk]
        part = []
        for c in range(2):
            dims = slice(c * DIFF_HEAD_DIM, (c + 1) * DIFF_HEAD_DIM)
            s = _dot(kc[:, dims], qts[c][dims, :])
            s_bufs[c][...] = s
            part.append(_col_part(s, jnp.max))
        return part

    def new_max(m, part):
        m_new = [jnp.maximum(m[c], jnp.max(part[c], axis=0, keepdims=True)) for c in range(2)]
        return m_new, [jnp.exp2(m[c] - m_new[c]) for c in range(2)]

    def expo(s_bufs, p_bufs, m):
        for c in range(2):
            p_bufs[c][...] = jnp.exp2((s_bufs[c][...] - m[c]).astype(BF16))

    def pv(p_bufs, chunk, alpha):
        vc = vt_ref[0, chunk, 0]
        for c in range(2):
            acc_refs[c][...] = acc_refs[c][...] * alpha[c] + _dot(vc, p_bufs[c][...])

    def finish(qb):
        o = (acc0_ref[0:128] / acc0_ref[128:129]
             - lam * (acc1_ref[0:128] / acc1_ref[128:129]))
        o = o * lax.rsqrt(jnp.mean(o * o, axis=0, keepdims=True) + SUBLN_EPS)
        o_ref[0, qb] = (o * g_ref[...] * (1.0 - LAMBDA_INIT)).astype(BF16)

    def handover(qb, cur):
        nxt = jnp.minimum(qb + 1, n_q - 1)
        qts = masked_q(nxt)
        part0 = qk(qts, 0, s_even)
        s_meta = [_dot(k_meta, qts[c]) for c in range(2)]
        if cur is not None:
            expo(s_odd, p_odd, cur[0])
            pv(p_even, n_chunks - 2, cur[2])
        m0 = [jnp.maximum(jnp.max(part0[c], axis=0, keepdims=True),
                          jnp.max(s_meta[c], axis=0, keepdims=True)) for c in range(2)]
        part1 = qk(qts, 1, s_odd)
        expo(s_even, p_even, m0)
        if cur is not None:
            pv(p_odd, n_chunks - 1, cur[1])
            finish(qb)
        for c in range(2):
            acc_refs[c][...] = _dot(vt_meta, jnp.exp2((s_meta[c] - m0[c]).astype(BF16)))
        m1, a1 = new_max(m0, part1)
        return tuple(m1), tuple(a1), tuple(jnp.ones_like(x) for x in m1)

    def steps(qts, t, m_e, a_e, a_p):
        part = qk(qts, t + 2, s_even)
        expo(s_odd, p_odd, m_e)
        pv(p_even, t, a_p)
        m_2, a_2 = new_max(m_e, part)
        part = qk(qts, t + 3, s_odd)
        expo(s_even, p_even, m_2)
        pv(p_odd, t + 1, a_e)
        m_3, a_3 = new_max(m_2, part)
        return tuple(m_3), tuple(a_3), tuple(a_2)

    def block(qb, carry):
        qts = masked_q(qb)
        for i in range(n_chunks // 2 - 1):
            carry = steps(qts, 2 * i, *carry)
        return handover(qb, carry)

    assert n_chunks % 2 == 0 and n_chunks >= 4

    def two_blocks(i, carry):
        return block(2 * i + 1, block(2 * i, carry))

    lax.fori_loop(0, n_q // 2, two_blocks, handover(-1, None))


def _df_attention(qt, k, vt, k_meta, vt_meta, lam_vecs, subln_col):
    b, n_chunks, _, _ = qt.shape
    k4 = k.reshape(b, n_chunks, CHUNK, DIFF_WIDTH)
    return pl.pallas_call(
        functools.partial(_df_kernel, n_chunks=n_chunks),
        grid=(b, DIFF_HEADS),
        in_specs=[
            pl.BlockSpec((1, n_chunks, 128, CHUNK), lambda bi, h: (bi, 0, h, 0)),
            pl.BlockSpec((1, n_chunks, CHUNK, 128), lambda bi, h: (bi, 0, 0, h)),
            pl.BlockSpec((1, n_chunks, 1, V_ROWS, CHUNK), lambda bi, h: (bi, 0, h, 0, 0)),
            pl.BlockSpec((N_META, 128), lambda bi, h: (0, h)),
            pl.BlockSpec((1, V_ROWS, N_META), lambda bi, h: (h, 0, 0)),
            pl.BlockSpec((4, DIFF_HEAD_DIM), lambda bi, h: (0, 0)),
            pl.BlockSpec((128, 1), lambda bi, h: (0, 0)),
        ],
        out_specs=pl.BlockSpec((1, n_chunks, 128, CHUNK), lambda bi, h: (bi, 0, h, 0)),
        out_shape=jax.ShapeDtypeStruct(qt.shape, BF16),
        scratch_shapes=[pltpu.VMEM((CHUNK, CHUNK), F32)] * 4
        + [pltpu.VMEM((CHUNK, CHUNK), BF16)] * 4
        + [pltpu.VMEM((V_ROWS, CHUNK), F32)] * 2,
        compiler_params=pltpu.CompilerParams(
            dimension_semantics=("parallel", "parallel"), vmem_limit_bytes=VMEM_LIMIT_BYTES),
        name="df_attn",
    )(qt, k4, vt, k_meta, vt_meta, lam_vecs, subln_col)


def _out_ffn_kernel(x_ref, nat_ref, dft_ref, g_na_ref, g_df_ref, w_na_ref, w_df_ref, w_o_ref,
                    ffn_g_ref, w_gate_ref, w_up_ref, w_down_ref, fin_g_ref, o_ref):
    o_na = _dot_tn(nat_ref[0, 0], w_na_ref[...])
    o_df = _dot_tn(dft_ref[0, 0], w_df_ref[...])
    merged = g_na_ref[0].astype(F32) * o_na + g_df_ref[0].astype(F32) * o_df
    x1 = x_ref[0] + _dot(merged.astype(BF16), w_o_ref[...])
    h = (_rms_scale(x1, NORM_EPS) * ffn_g_ref[...]).astype(BF16)
    gate = _dot(h, w_gate_ref[...])
    up = _dot(h, w_up_ref[...])
    act = (gate * jax.nn.sigmoid(gate) * up).astype(BF16)
    x2 = x1 + _dot(act, w_down_ref[...])
    o_ref[0] = _rms_scale(x2, NORM_EPS) * fin_g_ref[...]


def _out_ffn(x, nat, dft, g_na, g_df, w_na, w_df, w_o, ffn_g, w_gate, w_up, w_down, fin_g, rows):
    b, t, _ = x.shape
    per_chunk = CHUNK // rows
    const = lambda shape: pl.BlockSpec(shape, lambda i, j: (0,) * len(shape),
                                       pipeline_mode=pl.Buffered(1))
    tok_spec = pl.BlockSpec((1, rows, D_MODEL), lambda i, j: (i, j, 0))
    feat_spec = pl.BlockSpec((1, 1, 512, rows), lambda i, j: (i, j // per_chunk, 0, j % per_chunk))
    return pl.pallas_call(
        _out_ffn_kernel,
        grid=(b, t // rows),
        in_specs=[tok_spec, feat_spec, feat_spec, tok_spec, tok_spec,
                  const((NA_WIDTH, D_MODEL)), const((DIFF_WIDTH, D_MODEL)), const((D_MODEL, D_MODEL)),
                  const((1, D_MODEL)), const((D_MODEL, D_FF)), const((D_MODEL, D_FF)),
                  const((D_FF, D_MODEL)), const((1, D_MODEL))],
        out_specs=tok_spec,
        out_shape=jax.ShapeDtypeStruct(x.shape, F32),
        compiler_params=pltpu.CompilerParams(
            dimension_semantics=("parallel", "parallel"), vmem_limit_bytes=VMEM_LIMIT_BYTES),
        name="out_ffn",
    )(x, nat, dft, g_na, g_df, w_na, w_df, w_o, ffn_g, w_gate, w_up, w_down, fin_g)


def kernel(x, meta_tokens, mix_norm, w_in, na_rpb, lambda_q1, lambda_k1, lambda_q2, lambda_k2,
           diff_subln, w_na_out, w_diff_out, w_o, ffn_norm, w_gate, w_up, w_down, final_norm):
    b, t, _ = x.shape
    assert mix_norm.shape[0] == 1, "single-layer block"
    assert t % CHUNK == 0 and t // CHUNK >= 3

    w = w_in[0].astype(BF16)
    na_q, na_k, na_v, df_q, df_k, df_v, g_na, g_df = jnp.split(
        w, [512, 1024, 1536, 2048, 2560, 3072, 4096], axis=1)
    w_t = jnp.concatenate([na_q, na_v, df_q, df_v], axis=1).T
    w_s = jnp.concatenate([na_k, df_k, g_na, g_df], axis=1)
    gain = mix_norm[0][None].astype(F32)

    pos = jnp.arange(N_META + t, dtype=jnp.int32)
    tables_meta = _rope_tables(pos[:N_META])
    tables_real = _rope_tables(pos[N_META:])

    na_qt, na_vt, df_qt, df_vt, na_k_r, df_k_r, g_na_r, g_df_r = _in_proj(
        x, gain, w_t, w_s, *tables_real, rows=2 * CHUNK)
    meta = _in_proj(meta_tokens[None].astype(x.dtype), gain, w_t, w_s, *tables_meta, rows=N_META)
    na_vt_m, df_vt_m, na_k_m, df_k_m = meta[1][0], meta[3][0, 0], meta[4][0], meta[5][0]

    bias = _na_bias_tiles(na_rpb[0], t // GRID_W)
    na_out_t = _na_attention(na_qt, na_k_r, na_vt, na_k_m, na_vt_m, bias)

    lam_vecs = jnp.stack([lambda_q1[0], lambda_k1[0], lambda_q2[0], lambda_k2[0]]).astype(F32)
    df_out_t = _df_attention(df_qt, df_k_r, df_vt, df_k_m, df_vt_m, lam_vecs,
                             diff_subln[0].astype(F32)[:, None])

    return _out_ffn(x, na_out_t, df_out_t, g_na_r, g_df_r,
                    w_na_out[0].astype(BF16), w_diff_out[0].astype(BF16), w_o[0].astype(BF16),
                    ffn_norm[0][None].astype(F32), w_gate[0].astype(BF16), w_up[0].astype(BF16),
                    w_down[0].astype(BF16), final_norm[None].astype(F32), rows=512)
```

```python
import functools
import math

import jax
import jax.numpy as jnp
from jax import lax
from jax.experimental import pallas as pl
from jax.experimental.pallas import tpu as pltpu

D_MODEL = 1024
GRID_W = 64
N_META = 16
NA_HEADS = 8
NA_HEAD_DIM = 64
NA_WIN_ROWS = 8
NA_WIN_COLS = 16
DIFF_HEADS = 4
DIFF_HEAD_DIM = 64
NA_WIDTH = NA_HEADS * NA_HEAD_DIM
DIFF_WIDTH = DIFF_HEADS * 2 * DIFF_HEAD_DIM
D_FF = 2816
ROPE_THETA = 10000.0
NORM_EPS = 1e-6
SUBLN_EPS = 1e-5
LAMBDA_INIT = 0.8 - 0.6 * math.exp(-0.3 * 0)

CHUNK = 512
NA_GROUP_ROWS = CHUNK // GRID_W
NA_KEY_ROWS = 16
NA_PAIR = 2 * GRID_W
NA_PAIR_KEY_ROWS = NA_WIN_ROWS + 2
NA_QUAD = 4
NA_QUAD_DIM = NA_QUAD * NA_HEAD_DIM
V_ROWS = 128 + 16
FFN_SPLIT = 1536
LOG2_E = math.log2(math.e)
MASK_VALUE = -1e30
VMEM_LIMIT_BYTES = 56 * 1024 * 1024

BF16 = jnp.bfloat16
F32 = jnp.float32


def _dot(a, b):
    return jnp.dot(a, b, preferred_element_type=F32)


def _dot_nt(a, b):
    return lax.dot_general(a, b, (((1,), (1,)), ((), ())), preferred_element_type=F32)


def _dot_tn(a, b):
    return lax.dot_general(a, b, (((0,), (0,)), ((), ())), preferred_element_type=F32)


def _rms_scale(x, eps):
    return x * lax.rsqrt(jnp.mean(x * x, axis=-1, keepdims=True) + eps)


def _in_proj_kernel(x_ref, g_ref, wt_ref, ws_ref, cos_t_ref, sin_t_ref, cos_s_ref, sin_s_ref,
                    na_qt_ref, na_vt_ref, df_qt_ref, df_vt_ref, na_k_ref, df_k_ref,
                    g_na_ref, g_df_ref):
    h = (_rms_scale(x_ref[0], NORM_EPS) * g_ref[...]).astype(BF16)

    g_na_ref[0] = jax.nn.sigmoid(_dot(h, ws_ref[:, 1024:2048])).astype(BF16)
    g_df_ref[0] = jax.nn.sigmoid(_dot(h, ws_ref[:, 2048:3072])).astype(BF16)

    n_ch, chunk = na_qt_ref.shape[1], na_qt_ref.shape[-1]
    lanes = [slice(ci * chunk, (ci + 1) * chunk) for ci in range(n_ch)]
    nq = (_dot_nt(wt_ref[0:512], h) * (LOG2_E * NA_HEAD_DIM ** -0.5)).astype(BF16)
    for ci in range(n_ch):
        na_qt_ref[0, ci] = nq[:, lanes[ci]]
    nv = _dot_nt(wt_ref[512:1024], h).astype(BF16)
    dv = _dot_nt(wt_ref[1536:2048], h).astype(BF16)
    pad_row = lax.broadcasted_iota(jnp.int32, (V_ROWS - 128, chunk), 0)
    ones_pad = jnp.where(pad_row == 0, 1.0, 0.0).astype(BF16)
    for ci in range(n_ch):
        for hd in range(DIFF_HEADS):
            df_vt_ref[0, ci, hd, 0:128, :] = dv[hd * 128:(hd + 1) * 128, lanes[ci]]
            df_vt_ref[0, ci, hd, 128:V_ROWS, :] = ones_pad
    pair = na_vt_ref.shape[-1]
    pad_row = lax.broadcasted_iota(jnp.int32, (V_ROWS - 128, pair), 0)
    ones_pair = jnp.where(pad_row == 0, 1.0, 0.0).astype(BF16)
    for hp in range(NA_HEADS // 2):
        for pr in range(nv.shape[1] // pair):
            na_vt_ref[0, hp, pr, 0:128, :] = nv[hp * 128:(hp + 1) * 128, pr * pair:(pr + 1) * pair]
            na_vt_ref[0, hp, pr, 128:V_ROWS, :] = ones_pair
    dq = _dot_nt(wt_ref[1024:1536], h) * (LOG2_E * DIFF_HEAD_DIM ** -0.5)
    cos_t = cos_t_ref[...]
    sin_t = sin_t_ref[...]
    half = DIFF_HEAD_DIM // 2
    for c in range(DIFF_WIDTH // DIFF_HEAD_DIM):
        x1 = dq[c * 64:c * 64 + half]
        x2 = dq[c * 64 + half:(c + 1) * 64]
        top = (x1 * cos_t - x2 * sin_t).astype(BF16)
        bottom = (x2 * cos_t + x1 * sin_t).astype(BF16)
        for ci in range(n_ch):
            df_qt_ref[0, ci, c * 64:c * 64 + half, :] = top[:, lanes[ci]]
            df_qt_ref[0, ci, c * 64 + half:(c + 1) * 64, :] = bottom[:, lanes[ci]]

    dk = _dot(h, ws_ref[:, 512:1024])
    cos_s = cos_s_ref[...]
    sin_s = sin_s_ref[...]
    lane = lax.broadcasted_iota(jnp.int32, cos_s.shape, 1)
    low_half = (lane % DIFF_HEAD_DIM) < half
    for c in range(DIFF_WIDTH // 128):
        xk = dk[:, c * 128:(c + 1) * 128]
        partner = jnp.where(low_half, pltpu.roll(xk, 128 - half, 1), pltpu.roll(xk, half, 1))
        df_k_ref[0, :, c * 128:(c + 1) * 128] = (xk * cos_s + partner * sin_s).astype(BF16)
    na_k_ref[0] = _dot(h, ws_ref[:, 0:512]).astype(BF16)


def _in_proj(x, gain, w_t, w_s, cos_t, sin_t, cos_s, sin_s, rows):
    b, t, _ = x.shape
    n = t // rows
    const = lambda shape: pl.BlockSpec(shape, lambda i, j: (0,) * len(shape),
                                       pipeline_mode=pl.Buffered(1))
    chunk = min(CHUNK, rows)
    n_ch = rows // chunk
    feat_major = jax.ShapeDtypeStruct((b, t // chunk, 512, chunk), BF16)
    feat_spec = pl.BlockSpec((1, n_ch, 512, chunk), lambda i, j: (i, j, 0, 0))
    pair = min(NA_PAIR, rows)
    value_major = jax.ShapeDtypeStruct((b, t // chunk, 4, V_ROWS, chunk), BF16)
    value_spec = pl.BlockSpec((1, n_ch, 4, V_ROWS, chunk), lambda i, j: (i, j, 0, 0, 0))
    tok = lambda width: jax.ShapeDtypeStruct((b, t, width), BF16)
    tok_spec = lambda width: pl.BlockSpec((1, rows, width), lambda i, j: (i, j, 0))
    return pl.pallas_call(
        _in_proj_kernel,
        grid=(b, n),
        in_specs=[
            pl.BlockSpec((1, rows, D_MODEL), lambda i, j: (i, j, 0)),
            const((1, D_MODEL)),
            const((2048, D_MODEL)),
            const((D_MODEL, 3072)),
            pl.BlockSpec((DIFF_HEAD_DIM // 2, rows), lambda i, j: (0, j)),
            pl.BlockSpec((DIFF_HEAD_DIM // 2, rows), lambda i, j: (0, j)),
            pl.BlockSpec((rows, 128), lambda i, j: (j, 0)),
            pl.BlockSpec((rows, 128), lambda i, j: (j, 0)),
        ],
        out_specs=[feat_spec,
                   pl.BlockSpec((1, NA_HEADS // 2, rows // pair, V_ROWS, pair),
                                lambda i, j: (i, 0, j, 0, 0)),
                   feat_spec, value_spec,
                   tok_spec(512), tok_spec(512), tok_spec(D_MODEL), tok_spec(D_MODEL)],
        out_shape=[feat_major,
                   jax.ShapeDtypeStruct((b, NA_HEADS // 2, t // pair, V_ROWS, pair), BF16),
                   feat_major, value_major,
                   tok(512), tok(512), tok(D_MODEL), tok(D_MODEL)],
        compiler_params=pltpu.CompilerParams(
            dimension_semantics=("parallel", "parallel"), vmem_limit_bytes=VMEM_LIMIT_BYTES),
        name="in_proj",
    )(x, gain, w_t, w_s, cos_t, sin_t, cos_s, sin_s)


def _rope_tables(pos):
    half = DIFF_HEAD_DIM // 2
    inv = ROPE_THETA ** (-jnp.arange(half, dtype=F32) / half)
    ang = pos.astype(F32)[:, None] * inv[None, :]
    cos, sin = jnp.cos(ang), jnp.sin(ang)
    cos_s = jnp.tile(cos, (1, 128 // half))
    sin_s = jnp.tile(jnp.concatenate([-sin, sin], axis=1), (1, 128 // DIFF_HEAD_DIM))
    return cos.T, sin.T, cos_s, sin_s


def _na_pair_base(first_row, n_rows):
    return jnp.clip(first_row - NA_WIN_ROWS // 2, 0, n_rows - NA_PAIR_KEY_ROWS)


def _na_pair_variants(n_rows):
    assert n_rows >= 2 * NA_PAIR_KEY_ROWS + 2
    kinds = []
    for first_row in (2 * NA_WIN_ROWS, 0, 2, n_rows - 4, n_rows - 2):
        base = min(max(first_row - NA_WIN_ROWS // 2, 0), n_rows - NA_PAIR_KEY_ROWS)
        rows = []
        for r in (first_row, first_row + 1):
            start = min(max(r - NA_WIN_ROWS // 2, 0), n_rows - NA_WIN_ROWS)
            rows.append((start - base, start - base + NA_WIN_ROWS, base - r))
        kinds.append(rows)
    return kinds


def _na_pair_kind(pair_index, n_pairs):
    return jnp.where(pair_index == 0, 1, jnp.where(pair_index == 1, 2, jnp.where(
        pair_index == n_pairs - 2, 3, jnp.where(pair_index == n_pairs - 1, 4, 0))))


def _na_bias_tiles(rpb, n_rows):
    n_heads, n_dr, n_dc = rpb.shape
    kinds = _na_pair_variants(n_rows)
    mid = NA_WIN_COLS - 1
    ring = jnp.concatenate([rpb[:, :, mid::-1].astype(F32),
                            jnp.zeros((n_heads, n_dr, 128 - n_dc), F32),
                            rpb[:, :, :mid:-1].astype(F32)], axis=-1)

    def body(ring_ref, o_ref):
        key_col = lax.broadcasted_iota(jnp.int32, (GRID_W, 128), 0)
        lane = lax.broadcasted_iota(jnp.int32, (GRID_W, 128), 1)
        col_start = jnp.clip(lane % GRID_W - NA_WIN_COLS // 2, 0, GRID_W - NA_WIN_COLS)
        col_ok = (key_col >= col_start) & (key_col < col_start + NA_WIN_COLS)
        masked = jnp.full((GRID_W, 128), MASK_VALUE, F32)
        tables = []
        for d in range(n_dr):
            x = jnp.broadcast_to(ring_ref[0, d:d + 1, :], (GRID_W, 128)) * LOG2_E
            for bit in range(GRID_W.bit_length() - 1):
                x = jnp.where((key_col >> bit) & 1 == 1, pltpu.roll(x, 1 << bit, 1), x)
            x = jnp.where(lane < GRID_W, x, pltpu.roll(x, GRID_W, 1))
            tables.append(jnp.where(col_ok, x, masked))
        for v, kind in enumerate(kinds):
            for i in range(NA_PAIR_KEY_ROWS):
                halves = [tables[shift + i + NA_WIN_ROWS - 1] if lo <= i < hi else masked
                          for lo, hi, shift in kind]
                o_ref[v, 0, i * GRID_W:(i + 1) * GRID_W, :] = jnp.where(
                    lane < GRID_W, halves[0], halves[1])

    n_keys = NA_PAIR_KEY_ROWS * GRID_W
    return pl.pallas_call(
        body,
        grid=(n_heads,),
        in_specs=[pl.BlockSpec((1, n_dr, 128), lambda h: (h, 0, 0))],
        out_specs=pl.BlockSpec((len(kinds), 1, n_keys, NA_PAIR),
                               lambda h: (0, h // NA_QUAD, 0, h % NA_QUAD)),
        out_shape=jax.ShapeDtypeStruct((len(kinds), n_heads // NA_QUAD, n_keys, NA_QUAD * NA_PAIR),
                                       F32),
        compiler_params=pltpu.CompilerParams(
            dimension_semantics=("parallel",), vmem_limit_bytes=VMEM_LIMIT_BYTES),
        name="na_bias",
    )(ring)


def _na_window_row(g, n_rows):
    quarter = NA_KEY_ROWS // 4
    return jnp.clip(2 * g - 1, 0, (n_rows - NA_KEY_ROWS) // quarter) * quarter


def _na_kernel(qt_ref, k_ref, vt_ref, km_ref, vm_ref, bias_ref, o_ref, *s_refs, n_rows):
    g = pl.program_id(1)
    n_pairs = n_rows // 2
    window_row = _na_window_row(g, n_rows)
    n_keys = NA_PAIR_KEY_ROWS * GRID_W
    head_of_row = lax.broadcasted_iota(jnp.int32, (NA_QUAD_DIM, NA_PAIR), 0) // NA_HEAD_DIM
    n_quads = NA_HEADS // NA_QUAD
    units = [(jp, quad) for jp in range(NA_GROUP_ROWS // 2) for quad in range(n_quads)]

    def local_row(jp):
        return _na_pair_base(2 * (g * (NA_GROUP_ROWS // 2) + jp), n_rows) - window_row

    def scores(u):
        jp, quad = units[u]
        kind = _na_pair_kind(g * (NA_GROUP_ROWS // 2) + jp, n_pairs)
        key_off = pl.multiple_of(local_row(jp) * GRID_W, NA_PAIR)
        halves, meta_halves = [], []
        for hp in range(NA_QUAD // 2):
            feat = slice(quad * NA_QUAD_DIM + hp * 128, quad * NA_QUAD_DIM + (hp + 1) * 128)
            q_pair = qt_ref[0, 0, feat, jp * NA_PAIR:(jp + 1) * NA_PAIR]
            zero = jnp.zeros_like(q_pair)
            w = jnp.concatenate([jnp.where(head_of_row[0:128] == hd, q_pair, zero) for hd in range(2)],
                                axis=1)
            halves.append(_dot(k_ref[0, pl.ds(key_off, n_keys), feat], w))
            meta_halves.append(_dot(km_ref[:, feat], w))
        s = jnp.concatenate(halves, axis=1) + bias_ref[kind, quad]
        s_refs[u][...] = s
        s_meta = jnp.concatenate(meta_halves, axis=1)
        m = jnp.maximum(jnp.max(_col_part(s, jnp.max), axis=0, keepdims=True),
                        jnp.max(s_meta, axis=0, keepdims=True))
        return s_meta, m

    def values(u, s_meta, m):
        jp, quad = units[u]
        slab_off = local_row(jp) // 2
        for hp in range(NA_QUAD // 2):
            slab = (NA_QUAD // 2) * quad + hp
            vt = jnp.concatenate([vt_ref[0, slab, slab_off + i] for i in range(NA_PAIR_KEY_ROWS // 2)],
                                 axis=1)
            cols = slice(2 * hp * NA_PAIR, 2 * (hp + 1) * NA_PAIR)
            p = jnp.exp2((s_refs[u][:, cols] - m[:, cols]).astype(BF16))
            p_meta = jnp.exp2((s_meta[:, cols] - m[:, cols]).astype(BF16))
            o = _dot(vt, p) + _dot(vm_ref[slab, 0], p_meta)
            denom = o[128:129]
            for hd in range(2):
                first = quad * NA_QUAD_DIM + (2 * hp + hd) * NA_HEAD_DIM
                o_ref[0, 0, first:first + NA_HEAD_DIM, jp * NA_PAIR:(jp + 1) * NA_PAIR] = (
                    o[hd * NA_HEAD_DIM:(hd + 1) * NA_HEAD_DIM, hd * NA_PAIR:(hd + 1) * NA_PAIR]
                    / denom[:, hd * NA_PAIR:(hd + 1) * NA_PAIR]).astype(BF16)

    pending = scores(0)
    for u in range(len(units)):
        following = scores(u + 1) if u + 1 < len(units) else None
        values(u, *pending)
        pending = following


def _na_attention(qt, k, vt, k_meta, vt_meta, bias):
    b, n_groups, _, _ = qt.shape
    n_rows = n_groups * NA_GROUP_ROWS
    n_quads = NA_HEADS // NA_QUAD
    n_kinds, _, n_keys, _ = bias.shape
    return pl.pallas_call(
        functools.partial(_na_kernel, n_rows=n_rows),
        grid=(b, n_groups),
        in_specs=[
            pl.BlockSpec((1, 1, NA_WIDTH, CHUNK), lambda bi, g: (bi, g, 0, 0)),
            pl.BlockSpec((pl.Element(1), pl.Element(NA_KEY_ROWS * GRID_W), pl.Element(NA_WIDTH)),
                         lambda bi, g: (bi, _na_window_row(g, n_rows) * GRID_W, 0)),
            pl.BlockSpec((pl.Element(1), pl.Element(NA_HEADS // 2), pl.Element(NA_KEY_ROWS // 2),
                          pl.Element(V_ROWS), pl.Element(NA_PAIR)),
                         lambda bi, g: (bi, 0, _na_window_row(g, n_rows) // 2, 0, 0)),
            pl.BlockSpec((N_META, NA_WIDTH), lambda bi, g: (0, 0)),
            pl.BlockSpec((NA_HEADS // 2, 1, V_ROWS, N_META), lambda bi, g: (0, 0, 0, 0)),
            pl.BlockSpec((n_kinds, n_quads, n_keys, NA_QUAD * NA_PAIR), lambda bi, g: (0, 0, 0, 0),
                         pipeline_mode=pl.Buffered(1)),
        ],
        out_specs=pl.BlockSpec((1, 1, NA_WIDTH, CHUNK), lambda bi, g: (bi, g, 0, 0)),
        out_shape=jax.ShapeDtypeStruct(qt.shape, BF16),
        scratch_shapes=[pltpu.VMEM((n_keys, NA_QUAD * NA_PAIR), F32)]
        * (n_quads * NA_GROUP_ROWS // 2),
        compiler_params=pltpu.CompilerParams(
            dimension_semantics=("parallel", "parallel"), vmem_limit_bytes=VMEM_LIMIT_BYTES),
        name="na_attn",
    )(qt, k, vt, k_meta, vt_meta, bias)


def _col_part(x, op):
    return op(x.reshape(x.shape[0] // 8, 8, x.shape[1]), axis=0)


def _df_kernel(qt_ref, k_ref, vt_ref, km_ref, vm_ref, lam_ref, g_ref, o_ref,
               s_a0, s_a1, s_b0, s_b1, p_a0, p_a1, p_b0, p_b1, acc0_ref, acc1_ref, *, n_chunks):
    s_even, s_odd = (s_a0, s_a1), (s_b0, s_b1)
    p_even, p_odd = (p_a0, p_a1), (p_b0, p_b1)
    acc_refs = (acc0_ref, acc1_ref)
    n_q = n_chunks
    k_meta = km_ref[...]
    vt_meta = vm_ref[0]
    lam_v = lam_ref[...]
    lam = (jnp.exp(jnp.sum(lam_v[0:1] * lam_v[1:2], axis=1, keepdims=True))
           - jnp.exp(jnp.sum(lam_v[2:3] * lam_v[3:4], axis=1, keepdims=True)) + LAMBDA_INIT)

    def masked_q(qb):
        qt = qt_ref[0, qb]
        row = lax.broadcasted_iota(jnp.int32, qt.shape, 0)
        zero = jnp.zeros_like(qt)
        return (jnp.where(row < DIFF_HEAD_DIM, qt, zero), jnp.where(row >= DIFF_HEAD_DIM, qt, zero))

    def qk(qts, chunk, s_bufs):
        kc = k_ref[0, chunk]
        part = []
        for c in range(2):
            dims = slice(c * DIFF_HEAD_DIM, (c + 1) * DIFF_HEAD_DIM)
            s = _dot(kc[:, dims], qts[c][dims, :])
            s_bufs[c][...] = s
            part.append(_col_part(s, jnp.max))
        return part

    def new_max(m, part):
        m_new = [jnp.maximum(m[c], jnp.max(part[c], axis=0, keepdims=True)) for c in range(2)]
        return m_new, [jnp.exp2(m[c] - m_new[c]) for c in range(2)]

    def expo(s_bufs, p_bufs, m):
        for c in range(2):
            p_bufs[c][...] = jnp.exp2((s_bufs[c][...] - m[c]).astype(BF16))

    def pv(p_bufs, chunk, alpha):
        vc = vt_ref[0, chunk, 0]
        for c in range(2):
            acc_refs[c][...] = acc_refs[c][...] * alpha[c] + _dot(vc, p_bufs[c][...])

    def finish(qb):
        o = (acc0_ref[0:128] / acc0_ref[128:129]
             - lam * (acc1_ref[0:128] / acc1_ref[128:129]))
        o = o * lax.rsqrt(jnp.mean(o * o, axis=0, keepdims=True) + SUBLN_EPS)
        o_ref[0, qb] = (o * g_ref[...] * (1.0 - LAMBDA_INIT)).astype(BF16)

    def handover(qb, cur):
        nxt = jnp.minimum(qb + 1, n_q - 1)
        qts = masked_q(nxt)
        part0 = qk(qts, 0, s_even)
        s_meta = [_dot(k_meta, qts[c]) for c in range(2)]
        if cur is not None:
            expo(s_odd, p_odd, cur[0])
            pv(p_even, n_chunks - 2, cur[2])
        m0 = [jnp.maximum(jnp.max(part0[c], axis=0, keepdims=True),
                          jnp.max(s_meta[c], axis=0, keepdims=True)) for c in range(2)]
        part1 = qk(qts, 1, s_odd)
        expo(s_even, p_even, m0)
        if cur is not None:
            pv(p_odd, n_chunks - 1, cur[1])
            finish(qb)
        for c in range(2):
            acc_refs[c][...] = _dot(vt_meta, jnp.exp2((s_meta[c] - m0[c]).astype(BF16)))
        m1, a1 = new_max(m0, part1)
        return tuple(m1), tuple(a1), tuple(jnp.ones_like(x) for x in m1)

    def steps(qts, t, m_e, a_e, a_p):
        part = qk(qts, t + 2, s_even)
        expo(s_odd, p_odd, m_e)
        pv(p_even, t, a_p)
        m_2, a_2 = new_max(m_e, part)
        part = qk(qts, t + 3, s_odd)
        expo(s_even, p_even, m_2)
        pv(p_odd, t + 1, a_e)
        m_3, a_3 = new_max(m_2, part)
        return tuple(m_3), tuple(a_3), tuple(a_2)

    def block(qb, carry):
        qts = masked_q(qb)
        for i in range(n_chunks // 2 - 1):
            carry = steps(qts, 2 * i, *carry)
        return handover(qb, carry)

    assert n_chunks % 2 == 0 and n_chunks >= 4
    lax.fori_loop(0, n_q, block, handover(-1, None))


def _df_attention(qt, k, vt, k_meta, vt_meta, lam_vecs, subln_col):
    b, n_chunks, _, _ = qt.shape
    k4 = k.reshape(b, n_chunks, CHUNK, DIFF_WIDTH)
    return pl.pallas_call(
        functools.partial(_df_kernel, n_chunks=n_chunks),
        grid=(b, DIFF_HEADS),
        in_specs=[
            pl.BlockSpec((1, n_chunks, 128, CHUNK), lambda bi, h: (bi, 0, h, 0)),
            pl.BlockSpec((1, n_chunks, CHUNK, 128), lambda bi, h: (bi, 0, 0, h)),
            pl.BlockSpec((1, n_chunks, 1, V_ROWS, CHUNK), lambda bi, h: (bi, 0, h, 0, 0)),
            pl.BlockSpec((N_META, 128), lambda bi, h: (0, h)),
            pl.BlockSpec((1, V_ROWS, N_META), lambda bi, h: (h, 0, 0)),
            pl.BlockSpec((4, DIFF_HEAD_DIM), lambda bi, h: (0, 0)),
            pl.BlockSpec((128, 1), lambda bi, h: (0, 0)),
        ],
        out_specs=pl.BlockSpec((1, n_chunks, 128, CHUNK), lambda bi, h: (bi, 0, h, 0)),
        out_shape=jax.ShapeDtypeStruct(qt.shape, BF16),
        scratch_shapes=[pltpu.VMEM((CHUNK, CHUNK), F32)] * 4
        + [pltpu.VMEM((CHUNK, CHUNK), BF16)] * 4
        + [pltpu.VMEM((V_ROWS, CHUNK), F32)] * 2,
        compiler_params=pltpu.CompilerParams(
            dimension_semantics=("parallel", "parallel"), vmem_limit_bytes=VMEM_LIMIT_BYTES),
        name="df_attn",
    )(qt, k4, vt, k_meta, vt_meta, lam_vecs, subln_col)


def _out_ffn_kernel(x_ref, nat_ref, dft_ref, g_na_ref, g_df_ref, w_na_ref, w_df_ref, w_o_ref,
                    ffn_g_ref, w_gate_ref, w_up_ref, w_down_ref, fin_g_ref, o_ref):
    o_na = _dot_tn(nat_ref[0, 0], w_na_ref[...])
    o_df = _dot_tn(dft_ref[0, 0], w_df_ref[...])
    merged = g_na_ref[0].astype(F32) * o_na + g_df_ref[0].astype(F32) * o_df
    x1 = x_ref[0] + _dot(merged.astype(BF16), w_o_ref[...])
    h = (_rms_scale(x1, NORM_EPS) * ffn_g_ref[...]).astype(BF16)
    x2 = x1
    for cols in (slice(0, FFN_SPLIT), slice(FFN_SPLIT, D_FF)):
        gate = _dot(h, w_gate_ref[:, cols])
        up = _dot(h, w_up_ref[:, cols])
        act = (gate * jax.nn.sigmoid(gate) * up).astype(BF16)
        x2 = x2 + _dot(act, w_down_ref[cols, :])
    o_ref[0] = _rms_scale(x2, NORM_EPS) * fin_g_ref[...]


def _out_ffn(x, nat, dft, g_na, g_df, w_na, w_df, w_o, ffn_g, w_gate, w_up, w_down, fin_g, rows):
    b, t, _ = x.shape
    per_chunk = CHUNK // rows
    const = lambda shape: pl.BlockSpec(shape, lambda i, j: (0,) * len(shape),
                                       pipeline_mode=pl.Buffered(1))
    tok_spec = pl.BlockSpec((1, rows, D_MODEL), lambda i, j: (i, j, 0))
    feat_spec = pl.BlockSpec((1, 1, 512, rows), lambda i, j: (i, j // per_chunk, 0, j % per_chunk))
    return pl.pallas_call(
        _out_ffn_kernel,
        grid=(b, t // rows),
        in_specs=[tok_spec, feat_spec, feat_spec, tok_spec, tok_spec,
                  const((NA_WIDTH, D_MODEL)), const((DIFF_WIDTH, D_MODEL)), const((D_MODEL, D_MODEL)),
                  const((1, D_MODEL)), const((D_MODEL, D_FF)), const((D_MODEL, D_FF)),
                  const((D_FF, D_MODEL)), const((1, D_MODEL))],
        out_specs=tok_spec,
        out_shape=jax.ShapeDtypeStruct(x.shape, F32),
        compiler_params=pltpu.CompilerParams(
            dimension_semantics=("parallel", "parallel"), vmem_limit_bytes=VMEM_LIMIT_BYTES),
        name="out_ffn",
    )(x, nat, dft, g_na, g_df, w_na, w_df, w_o, ffn_g, w_gate, w_up, w_down, fin_g)


def kernel(x, meta_tokens, mix_norm, w_in, na_rpb, lambda_q1, lambda_k1, lambda_q2, lambda_k2,
           diff_subln, w_na_out, w_diff_out, w_o, ffn_norm, w_gate, w_up, w_down, final_norm):
    b, t, _ = x.shape
    assert mix_norm.shape[0] == 1, "single-layer block"
    assert t % CHUNK == 0 and t // CHUNK >= 3

    w = w_in[0].astype(BF16)
    na_q, na_k, na_v, df_q, df_k, df_v, g_na, g_df = jnp.split(
        w, [512, 1024, 1536, 2048, 2560, 3072, 4096], axis=1)
    w_t = jnp.concatenate([na_q, na_v, df_q, df_v], axis=1).T
    w_s = jnp.concatenate([na_k, df_k, g_na, g_df], axis=1)
    gain = mix_norm[0][None].astype(F32)

    pos = jnp.arange(N_META + t, dtype=jnp.int32)
    tables_meta = _rope_tables(pos[:N_META])
    tables_real = _rope_tables(pos[N_META:])

    na_qt, na_vt, df_qt, df_vt, na_k_r, df_k_r, g_na_r, g_df_r = _in_proj(
        x, gain, w_t, w_s, *tables_real, rows=2 * CHUNK)
    meta = _in_proj(meta_tokens[None].astype(x.dtype), gain, w_t, w_s, *tables_meta, rows=N_META)
    na_vt_m, df_vt_m, na_k_m, df_k_m = meta[1][0], meta[3][0, 0], meta[4][0], meta[5][0]

    bias = _na_bias_tiles(na_rpb[0], t // GRID_W)
    na_out_t = _na_attention(na_qt, na_k_r, na_vt, na_k_m, na_vt_m, bias)

    lam_vecs = jnp.stack([lambda_q1[0], lambda_k1[0], lambda_q2[0], lambda_k2[0]]).astype(F32)
    df_out_t = _df_attention(df_qt, df_k_r, df_vt, df_k_m, df_vt_m, lam_vecs,
                             diff_subln[0].astype(F32)[:, None])

    return _out_ffn(x, na_out_t, df_out_t, g_na_r, g_df_r,
                    w_na_out[0].astype(BF16), w_diff_out[0].astype(BF16), w_o[0].astype(BF16),
                    ffn_norm[0][None].astype(F32), w_gate[0].astype(BF16), w_up[0].astype(BF16),
                    w_down[0].astype(BF16), final_norm[None].astype(F32), rows=512)
```
